```python
import math
import jax
import jax.numpy as jnp
from jax import lax
import numpy as np

D_MODEL = 4096
BATCH = 4
SEQ = 2048
DEPTH = 2
DEC_BATCH = 8
DEC_SEQ = 4
PAST_LEN = 16384
PAGE_SIZE = 128

H_A = D_MODEL // 512
DK_A = 128
DV_A = 128
W_A = H_A * DV_A
CONV_W = 4
GDN_CHUNK = 64
H_B = D_MODEL // 256
KV_B = 4
GRP_B = H_B // KV_B
DH_B = 128
W_B = H_B * DH_B
CMP_BLK = 32
SLC_BLK = 64
N_SEL = 16
WINDOW = 512
WIN_QBLK = 128
SLC_QBLK = 16
SEL_FORCE = 1e9
H_C = D_MODEL // 512
DK_C = 64
DV_C = 128
W_C = H_C * DV_C
MLSTM_CHUNK = 64
GATE_CAP = 15.0
REL_BUCKETS = 32
REL_EXACT = REL_BUCKETS // 2
REL_MAX_DIST = 128
N_EXPERTS = 32
TOP_K = 4
D_FF = D_MODEL // 2
SWIGLU_ALPHA = 1.702
SWIGLU_LIMIT = 7.0
MOE_BLOCK = 128
MOE_BLOCK_SMALL = 8

RMS_EPS = 1e-6
NEG = -1e30

IN_SIZES = (3 * W_A, W_A, H_A, H_A,
            W_B, 6 * KV_B * DH_B, 3 * H_B,
            H_C * DK_C, H_C * DK_C, W_C, 2 * H_C, W_C,
            3 * D_MODEL)
IN_OFFSETS = tuple(int(c) for c in np.cumsum(IN_SIZES)[:-1])
IN_COLS = int(sum(IN_SIZES))

kernel_name = 'hybrid_gdn_nsa_mlstm_moe_step'


def rms_norm(x, g):
    xf = x.astype(jnp.float32)
    y = xf * lax.rsqrt(jnp.mean(xf * xf, axis=-1, keepdims=True) + RMS_EPS)
    return (y * g.astype(jnp.float32)).astype(x.dtype)


def head_rms(x, w):
    return x * lax.rsqrt(jnp.mean(x * x, axis=-1, keepdims=True) + RMS_EPS) * w.astype(jnp.float32)


def l2_normalize(x):
    return x * lax.rsqrt(jnp.sum(x * x, axis=-1, keepdims=True) + RMS_EPS)


def softcap(x, cap):
    return cap * jnp.tanh(x / cap)


def masked_softmax(s, mask):
    p = jax.nn.softmax(jnp.where(mask, s, NEG), axis=-1)
    return jnp.where(mask, p, 0.0)


def rel_bucket(dist):
    n = jnp.maximum(dist, 0)
    nf = jnp.maximum(n, 1).astype(jnp.float32)
    ratio = jnp.log(nf / REL_EXACT) / math.log(REL_MAX_DIST / REL_EXACT)
    large = REL_EXACT + (ratio * (REL_BUCKETS - REL_EXACT)).astype(jnp.int32)
    return jnp.where(n < REL_EXACT, n, jnp.minimum(large, REL_BUCKETS - 1))


def rel_bias_heads(dist, table):
    b = table[rel_bucket(dist)].astype(jnp.float32).reshape(dist.shape + (KV_B, GRP_B))
    nd = dist.ndim
    return jnp.transpose(b, (nd, nd + 1) + tuple(range(nd)))


def pad_time(x, n_pad, value=0.0):
    if n_pad == 0:
        return x
    widths = [(0, 0)] * x.ndim
    widths[1] = (0, n_pad)
    return jnp.pad(x, widths, constant_values=value)


def to_chunks(x, chunk):
    B, T = x.shape[:2]
    x = x.reshape((B, T // chunk, chunk) + x.shape[2:])
    return jnp.moveaxis(x, (1, 3), (0, 2))


def from_chunks(x):
    x = jnp.moveaxis(x, (0, 2), (1, 3))
    return x.reshape((x.shape[0], x.shape[1] * x.shape[2]) + x.shape[3:])


def gdn_recurrence(q, k, v, g, beta, S0):
    T = q.shape[1]
    L = GDN_CHUNK
    n_pad = (-T) % L
    q, k, v, g, beta = [to_chunks(pad_time(t, n_pad), L) for t in (q, k, v, g, beta)]
    incl = jnp.tril(jnp.ones((L, L), bool))
    strict = jnp.tril(jnp.ones((L, L), bool), -1)
    eye = jnp.eye(L, dtype=jnp.float32)

    def step(S, inp):
        qi, ki, vi, gi, bi = inp
        gcum = jnp.cumsum(gi, axis=-1)
        decay = jnp.exp(jnp.where(incl, gcum[..., :, None] - gcum[..., None, :], NEG))
        kb = ki * bi[..., None]
        lower = jnp.where(strict, jnp.einsum('bhid,bhjd->bhij', kb, ki) * decay, 0.0)
        tinv = lax.linalg.triangular_solve(eye + lower, jnp.broadcast_to(eye, lower.shape),
                                           left_side=True, lower=True)
        u = jnp.einsum('bhij,bhjd->bhid', tinv, vi * bi[..., None])
        w = jnp.einsum('bhij,bhjd->bhid', tinv, kb * jnp.exp(gcum)[..., None])
        v_new = u - jnp.einsum('bhid,bhde->bhie', w, S)
        attn = jnp.where(incl, jnp.einsum('bhid,bhjd->bhij', qi, ki) * decay, 0.0)
        o = (jnp.einsum('bhid,bhde->bhie', qi * jnp.exp(gcum)[..., None], S)
             + jnp.einsum('bhij,bhje->bhie', attn, v_new))
        g_last = gcum[..., -1]
        S = (S * jnp.exp(g_last)[..., None, None]
             + jnp.einsum('bhid,bhie->bhde', ki * jnp.exp(g_last[..., None] - gcum)[..., None], v_new))
        return S, o

    S, o = lax.scan(step, S0, (q, k, v, g, beta))
    return from_chunks(o)[:, :T], S


def gdn_mixer(qkv_raw, z, a, b, conv_prev, S0, conv_w, a_log, dt_bias, norm_w):
    B, T, _ = qkv_raw.shape
    xp = jnp.concatenate([conv_prev.astype(qkv_raw.dtype), qkv_raw], axis=1)
    conv = sum(xp[:, j:j + T] * conv_w[j] for j in range(CONV_W))
    qkv = jax.nn.silu(conv.astype(jnp.float32)).reshape(B, T, 3, H_A, DK_A)
    q = l2_normalize(qkv[:, :, 0]) * DK_A ** -0.5
    k = l2_normalize(qkv[:, :, 1])
    v = qkv[:, :, 2]
    beta = jax.nn.sigmoid(b.astype(jnp.float32))
    g = -jnp.exp(a_log.astype(jnp.float32)) * jax.nn.softplus(a.astype(jnp.float32) + dt_bias.astype(jnp.float32))
    o, S = gdn_recurrence(q, k, v, g, beta, S0.astype(jnp.float32))
    o = head_rms(o, norm_w) * jax.nn.silu(z.astype(jnp.float32).reshape(B, T, H_A, DV_A))
    return o.reshape(B, T, W_A).astype(qkv_raw.dtype), S, xp[:, T:]


def mlstm_recurrence(q, k, v, ig, lf, C0, n0, m0):
    T = q.shape[1]
    L = MLSTM_CHUNK
    n_pad = (-T) % L
    q, k, v, lf = [to_chunks(pad_time(t, n_pad), L) for t in (q, k, v, lf)]
    ig = to_chunks(pad_time(ig, n_pad, NEG), L)
    incl = jnp.tril(jnp.ones((L, L), bool))

    def step(carry, inp):
        C, n, m = carry
        qi, ki, vi, ii, fi = inp
        b = jnp.cumsum(fi, axis=-1)
        dmat = jnp.where(incl, b[..., :, None] - b[..., None, :] + ii[..., None, :], NEG)
        inter = b + m[..., None]
        m_tok = jnp.maximum(jnp.max(dmat, axis=-1), inter)
        s = jnp.einsum('bhid,bhjd->bhij', qi, ki) * jnp.exp(dmat - m_tok[..., None])
        w_inter = jnp.exp(inter - m_tok)
        num = w_inter[..., None] * jnp.einsum('bhid,bhde->bhie', qi, C) + jnp.einsum('bhij,bhje->bhie', s, vi)
        den = w_inter * jnp.einsum('bhid,bhd->bhi', qi, n) + jnp.sum(s, axis=-1)
        h = num / jnp.maximum(jnp.abs(den), jnp.exp(-m_tok))[..., None]
        b_last = b[..., -1]
        wk_log = b_last[..., None] - b + ii
        m_new = jnp.maximum(b_last + m, jnp.max(wk_log, axis=-1))
        wk = jnp.exp(wk_log - m_new[..., None])
        dec = jnp.exp(b_last + m - m_new)
        C = dec[..., None, None] * C + jnp.einsum('bhj,bhjd,bhje->bhde', wk, ki, vi)
        n = dec[..., None] * n + jnp.einsum('bhj,bhjd->bhd', wk, ki)
        return (C, n, m_new), h

    (C, n, m), h = lax.scan(step, (C0, n0, m0), (q, k, v, ig, lf))
    return from_chunks(h)[:, :T], C, n, m


def mlstm_mixer(q_raw, k_raw, v_raw, if_raw, o_raw, C0, n0, m0, b_if, norm_w):
    B, T, _ = q_raw.shape
    f32 = jnp.float32
    q = q_raw.astype(f32).reshape(B, T, H_C, DK_C) * DK_C ** -0.5
    k = k_raw.astype(f32).reshape(B, T, H_C, DK_C)
    v = v_raw.astype(f32).reshape(B, T, H_C, DV_C)
    pre = softcap(if_raw.astype(f32).reshape(B, T, 2, H_C) + b_if.astype(f32), GATE_CAP)
    ig = pre[:, :, 0]
    lf = jax.nn.log_sigmoid(pre[:, :, 1])
    h, C, n, m = mlstm_recurrence(q, k, v, ig, lf, C0.astype(f32), n0.astype(f32), m0.astype(f32))
    h = head_rms(h, norm_w).reshape(B, T, W_C) * jax.nn.sigmoid(o_raw.astype(f32))
    return h.astype(q_raw.dtype), C, n, m


def nsa_compressed(q, qpos, kv_full, pe, w, table):
    B, Lk = kv_full.shape[:2]
    nb = Lk // CMP_BLK
    blocks = kv_full[:, :nb * CMP_BLK].reshape(B, nb, CMP_BLK, 2, KV_B, DH_B)
    pooled = jnp.mean(blocks * jnp.transpose(pe, (1, 0, 2))[None, None, :, :, None, :], axis=2)
    summ = jnp.einsum('bncgd,cde->bncge', pooled, w)
    end_pos = (jnp.arange(nb) + 1) * CMP_BLK - 1
    dist = qpos[:, None] - end_pos[None, :]
    s = (jnp.einsum('bgrqd,bngd->bgrqn', q, summ[:, :, 0]).astype(jnp.float32) * DH_B ** -0.5
         + rel_bias_heads(dist, table))
    p = masked_softmax(s, dist >= 0)
    o = jnp.einsum('bgrqn,bngd->bgrqd', p.astype(summ.dtype), summ[:, :, 1])
    return o, p


def nsa_select(p_cmp, qpos, total_len):
    n_slc = -(-total_len // SLC_BLK)
    ratio = SLC_BLK // CMP_BLK
    imp = jnp.sum(p_cmp, axis=2)
    imp = jnp.pad(imp, ((0, 0), (0, 0), (0, 0), (0, n_slc * ratio - imp.shape[-1])))
    imp = imp.reshape(imp.shape[:-1] + (n_slc, ratio)).sum(-1)
    blk = jnp.arange(n_slc)[None, :]
    cur = (qpos // SLC_BLK)[:, None]
    valid = blk <= cur
    forced = (blk == 0) | (blk == cur) | (blk == cur - 1)
    score = jnp.where(valid & forced, SEL_FORCE, jnp.where(valid, imp, -SEL_FORCE))
    _, idx = lax.top_k(score, min(N_SEL, n_slc))
    return idx


def nsa_selected(q, qpos, idx, fetch, table):
    B, _, _, Tq, _ = q.shape
    n_sel = idx.shape[-1]
    qb = math.gcd(Tq, SLC_QBLK)
    nq = Tq // qb
    q_ch = jnp.moveaxis(q.reshape(B, KV_B, GRP_B, nq, qb, DH_B), 3, 0)
    pos_ch = qpos.reshape(nq, qb)
    idx_ch = jnp.moveaxis(idx.reshape(B, KV_B, nq, qb, n_sel), 2, 0)
    table_g = table.reshape(REL_BUCKETS, KV_B, GRP_B)
    gi = jnp.arange(KV_B)[None, :, None, None]
    offs = jnp.arange(SLC_BLK)

    def one_block(args):
        qc, pc, ic = args
        kpos = (ic[..., None] * SLC_BLK + offs).reshape(B, KV_B, qb, n_sel * SLC_BLK)
        kc, vc = fetch(kpos)
        dist = pc[None, None, :, None] - kpos
        bias = jnp.moveaxis(table_g[rel_bucket(dist), gi].astype(jnp.float32), -1, 2)
        s = jnp.einsum('bgrqd,bgqkd->bgrqk', qc, kc).astype(jnp.float32) * DH_B ** -0.5 + bias
        p = masked_softmax(s, (dist >= 0)[:, :, None])
        return jnp.einsum('bgrqk,bgqkd->bgrqd', p.astype(vc.dtype), vc)

    o = lax.map(one_block, (q_ch, pos_ch, idx_ch))
    return jnp.moveaxis(o, 0, 3).reshape(B, KV_B, GRP_B, Tq, DH_B)


def nsa_window(q, qpos, kb, kpos, table):
    dist = qpos[:, :, None] - kpos[:, None, :]
    mask = (dist >= 0) & (dist < WINDOW) & (kpos[:, None, :] >= 0)
    s = (jnp.einsum('bgrnqd,bnkgd->bgrnqk', q, kb[:, :, :, 0]).astype(jnp.float32) * DH_B ** -0.5
         + rel_bias_heads(dist, table))
    p = masked_softmax(s, mask)
    return jnp.einsum('bgrnqk,bnkgd->bgrnqd', p.astype(kb.dtype), kb[:, :, :, 1])


def nsa_mixer(q_raw, qpos, gate_raw, kv_cmp_full, fetch, win_blocks, win_kpos, cmp_pe, cmp_w, table):
    B, Tq, _ = q_raw.shape
    q = q_raw.reshape(B, Tq, KV_B, GRP_B, DH_B).transpose(0, 2, 3, 1, 4)
    o_cmp, p_cmp = nsa_compressed(q, qpos, kv_cmp_full, cmp_pe, cmp_w, table)
    idx = nsa_select(p_cmp, qpos, kv_cmp_full.shape[1])
    o_slc = nsa_selected(q, qpos, idx, fetch, table)
    nq = win_blocks.shape[1]
    o_win = nsa_window(q.reshape(B, KV_B, GRP_B, nq, Tq // nq, DH_B), qpos.reshape(nq, Tq // nq),
                       win_blocks, win_kpos, table).reshape(B, KV_B, GRP_B, Tq, DH_B)
    g = jax.nn.sigmoid(gate_raw.astype(jnp.float32).reshape(B, Tq, 3, KV_B, GRP_B)).transpose(2, 0, 3, 4, 1)
    o = g[0][..., None] * o_cmp + g[1][..., None] * o_slc + g[2][..., None] * o_win
    return o.transpose(0, 3, 1, 2, 4).reshape(B, Tq, W_B).astype(q_raw.dtype)


def contiguous_fetch(kv):
    bi = jnp.arange(kv.shape[0])[:, None, None, None]
    gi = jnp.arange(KV_B)[None, :, None, None]

    def fetch(kpos):
        rows = kv[bi, kpos, :, gi]
        return rows[..., 0, :], rows[..., 1, :]
    return fetch


def paged_fetch(pool, l, page_table, kv_new):
    B, Tn = kv_new.shape[:2]
    bi = jnp.arange(B)[:, None, None, None]
    gi = jnp.arange(KV_B)[None, :, None, None]
    li = jnp.full((1, 1, 1, 1), l, jnp.int32)

    def fetch(kpos):
        in_past = kpos < PAST_LEN
        pc = jnp.minimum(kpos, PAST_LEN - 1)
        page = page_table[bi, pc // PAGE_SIZE]
        past_rows = pool[li, page, pc % PAGE_SIZE, :, gi].astype(kv_new.dtype)
        new_rows = kv_new[bi, jnp.clip(kpos - PAST_LEN, 0, Tn - 1), :, gi]
        rows = jnp.where(in_past[..., None, None], past_rows, new_rows)
        return rows[..., 0, :], rows[..., 1, :]
    return fetch


def moe_ffn(x, l, router_w, router_b, w_gate, b_gate, w_up, b_up, w_down, b_down):
    shape = x.shape
    xt = x.reshape(-1, D_MODEL)
    T = xt.shape[0]
    logits = jnp.dot(xt, router_w[l]).astype(jnp.float32) + router_b[l].astype(jnp.float32)
    top_val, top_exp = lax.top_k(logits, TOP_K)
    weights = jax.nn.softmax(top_val, axis=-1)
    n = T * TOP_K
    blk = MOE_BLOCK if n >= MOE_BLOCK * N_EXPERTS else MOE_BLOCK_SMALL
    flat_e = top_exp.reshape(-1)
    order = jnp.argsort(flat_e)
    e_sorted = flat_e[order]
    counts = jnp.zeros((N_EXPERTS,), jnp.int32).at[flat_e].add(1)
    padded = (counts + blk - 1) // blk * blk
    start = jnp.cumsum(counts) - counts
    pstart = jnp.cumsum(padded) - padded
    dest = pstart[e_sorted] + jnp.arange(n, dtype=jnp.int32) - start[e_sorted]
    n_blocks = -(-(n + N_EXPERTS * (blk - 1)) // blk)
    slot_tok = jnp.full((n_blocks * blk,), T, jnp.int32).at[dest].set((order // TOP_K).astype(jnp.int32))
    slot_w = jnp.zeros((n_blocks * blk,), jnp.float32).at[dest].set(weights.reshape(-1)[order])
    blk_exp = jnp.minimum(jnp.searchsorted(jnp.cumsum(padded), jnp.arange(n_blocks) * blk, side='right'),
                          N_EXPERTS - 1)
    x_ext = jnp.concatenate([xt, jnp.zeros((1, D_MODEL), xt.dtype)], axis=0)

    def run_block(args):
        tok, e = args
        xb = x_ext[tok]
        gate = jnp.minimum(jnp.dot(xb, w_gate[l, e]) + b_gate[l, e], SWIGLU_LIMIT)
        up = jnp.clip(jnp.dot(xb, w_up[l, e]) + b_up[l, e], -SWIGLU_LIMIT, SWIGLU_LIMIT)
        hb = (up + 1.0) * gate * jax.nn.sigmoid(SWIGLU_ALPHA * gate)
        return jnp.dot(hb, w_down[l, e]) + b_down[l, e]

    y_blk = lax.map(run_block, (slot_tok.reshape(n_blocks, blk), blk_exp))
    y = jnp.zeros((T + 1, D_MODEL), jnp.float32).at[slot_tok].add(
        y_blk.reshape(-1, D_MODEL).astype(jnp.float32) * slot_w[:, None])
    return y[:T].astype(x.dtype).reshape(shape)


def token_mixers(h, l, qpos, prm, past):
    B, T, _ = h.shape
    f32 = jnp.float32
    proj = jnp.dot(h, prm['w_in'][l])
    (a_qkv, a_z, a_a, a_b, b_q, b_kv, b_g, c_q, c_k, c_v, c_if, c_o, merge_g) = jnp.split(proj, IN_OFFSETS, axis=-1)
    if past is None:
        conv_prev = jnp.zeros((B, CONV_W - 1, 3 * W_A), h.dtype)
        S0 = jnp.zeros((B, H_A, DK_A, DV_A), f32)
        C0 = jnp.zeros((B, H_C, DK_C, DV_C), f32)
        n0 = jnp.zeros((B, H_C, DK_C), f32)
        m0 = jnp.full((B, H_C), NEG, f32)
    else:
        conv_prev = past['gdn_conv'][l]
        S0 = past['gdn_S'][l]
        C0 = past['mlstm_C'][l]
        n0 = past['mlstm_n'][l]
        m0 = past['mlstm_m'][l]

    o_a, S_a, conv_a = gdn_mixer(a_qkv, a_z, a_a, a_b, conv_prev, S0, prm['gdn_conv_w'][l],
                                 prm['gdn_a_log'][l], prm['gdn_dt_bias'][l], prm['gdn_norm_w'][l])

    kv_new = b_kv.reshape(B, T, 3, 2, KV_B, DH_B)
    kv_cmp, kv_slc, kv_win = kv_new[:, :, 0], kv_new[:, :, 1], kv_new[:, :, 2]
    if past is None:
        kv_cmp_full = kv_cmp
        fetch = contiguous_fetch(kv_slc)
        nq = T // WIN_QBLK
        span = WIN_QBLK + WINDOW
        kidx = jnp.arange(nq)[:, None] * WIN_QBLK + jnp.arange(span)[None, :]
        win_blocks = jnp.pad(kv_win, ((0, 0), (WINDOW, 0), (0, 0), (0, 0), (0, 0)))[:, kidx]
        win_kpos = kidx - WINDOW
        win_all = kv_win
    else:
        past_cmp = past['kv_cmp'][l, past['page_table']]
        kv_cmp_full = jnp.concatenate([past_cmp.reshape(B, -1, 2, KV_B, DH_B).astype(kv_cmp.dtype), kv_cmp], axis=1)
        fetch = paged_fetch(past['kv_slc'], l, past['page_table'], kv_slc)
        buf = past['kv_win'][l].astype(kv_win.dtype)
        wb = buf.shape[1]
        win_all = jnp.concatenate([buf, kv_win], axis=1)
        win_blocks = win_all[:, None]
        win_kpos = (PAST_LEN - wb + jnp.arange(wb + T))[None, :]
    o_b = nsa_mixer(b_q, qpos, b_g, kv_cmp_full, fetch, win_blocks, win_kpos,
                    prm['nsa_cmp_pe'][l], prm['nsa_cmp_w'][l], prm['rel_bias_table'])
    new_win = win_all[:, -min(WINDOW, win_all.shape[1]):]

    o_c, C, n, m = mlstm_mixer(c_q, c_k, c_v, c_if, c_o, C0, n0, m0, prm['mlstm_b_if'][l], prm['mlstm_norm_w'][l])

    g = jax.nn.sigmoid(merge_g.astype(f32).reshape(B, T, 3, D_MODEL))
    merged = (g[:, :, 0] * jnp.dot(o_a, prm['w_up_a'][l]) + g[:, :, 1] * jnp.dot(o_b, prm['w_up_b'][l])
              + g[:, :, 2] * jnp.dot(o_c, prm['w_up_c'][l]))
    out = jnp.dot(merged.astype(h.dtype), prm['w_out'][l])
    return out, (kv_cmp, kv_slc, new_win, S_a, conv_a, C, n, m)


def setup_inputs(seed: int = 0) -> dict:
    key = jax.random.key(seed)
    ks = iter(jax.random.split(key, 48))
    f32 = jnp.float32

    def nrm(shape, scale):
        return jax.random.normal(next(ks), shape, f32) * scale

    def gain(shape):
        return 1.0 + nrm(shape, 0.02)

    n_pages = PAST_LEN // PAGE_SIZE
    n_used = DEC_BATCH * n_pages
    n_pool = n_used + max(1, n_used // 4)
    win_buf = min(WINDOW, PAST_LEN)
    page_table = jax.random.permutation(next(ks), n_pool)[:n_used].reshape(DEC_BATCH, n_pages).astype(jnp.int32)
    dt = jnp.exp(jax.random.uniform(next(ks), (DEPTH, H_A), f32, math.log(1e-3), math.log(1e-1)))
    a_init = jax.random.uniform(next(ks), (DEPTH, H_A), f32, 1.0, 16.0)
    b_if = jnp.stack([nrm((DEPTH, H_C), 0.1), 3.0 + nrm((DEPTH, H_C), 0.1)], axis=1)
    return {
        'x_prompt': nrm((BATCH, SEQ, D_MODEL), 1.0),
        'x_sample': nrm((DEC_BATCH, DEC_SEQ, D_MODEL), 1.0),
        'cache_kv_cmp': nrm((DEPTH, n_pool, PAGE_SIZE, 2, KV_B, DH_B), 1.0),
        'cache_kv_slc': nrm((DEPTH, n_pool, PAGE_SIZE, 2, KV_B, DH_B), 1.0),
        'cache_kv_win': nrm((DEPTH, DEC_BATCH, win_buf, 2, KV_B, DH_B), 1.0),
        'state_gdn_S': nrm((DEPTH, DEC_BATCH, H_A, DK_A, DV_A), 0.1),
        'state_gdn_conv': nrm((DEPTH, DEC_BATCH, CONV_W - 1, 3 * W_A), 1.0),
        'state_mlstm_C': nrm((DEPTH, DEC_BATCH, H_C, DK_C, DV_C), 0.1),
        'state_mlstm_n': nrm((DEPTH, DEC_BATCH, H_C, DK_C), 0.1),
        'state_mlstm_m': nrm((DEPTH, DEC_BATCH, H_C), 1.0),
        'page_table': page_table,
        'norm_mix': gain((DEPTH, D_MODEL)),
        'norm_ffn': gain((DEPTH, D_MODEL)),
        'norm_final': gain((D_MODEL,)),
        'w_in': nrm((DEPTH, D_MODEL, IN_COLS), D_MODEL ** -0.5),
        'gdn_conv_w': nrm((DEPTH, CONV_W, 3 * W_A), 0.5),
        'gdn_a_log': jnp.log(a_init),
        'gdn_dt_bias': dt + jnp.log(-jnp.expm1(-dt)),
        'gdn_norm_w': gain((DEPTH, DV_A)),
        'nsa_cmp_pe': 1.0 + nrm((DEPTH, 2, CMP_BLK, DH_B), 0.1),
        'nsa_cmp_w': nrm((DEPTH, 2, DH_B, DH_B), DH_B ** -0.5),
        'rel_bias_table': nrm((REL_BUCKETS, H_B), 0.5),
        'mlstm_b_if': b_if,
        'mlstm_norm_w': gain((DEPTH, DV_C)),
        'w_up_a': nrm((DEPTH, W_A, D_MODEL), W_A ** -0.5),
        'w_up_b': nrm((DEPTH, W_B, D_MODEL), W_B ** -0.5),
        'w_up_c': nrm((DEPTH, W_C, D_MODEL), W_C ** -0.5),
        'w_out': nrm((DEPTH, D_MODEL, D_MODEL), 0.5 * D_MODEL ** -0.5),
        'router_w': nrm((DEPTH, D_MODEL, N_EXPERTS), D_MODEL ** -0.5),
        'router_b': nrm((DEPTH, N_EXPERTS), 0.01),
        'exp_w_gate': nrm((DEPTH, N_EXPERTS, D_MODEL, D_FF), D_MODEL ** -0.5),
        'exp_b_gate': nrm((DEPTH, N_EXPERTS, D_FF), 0.01),
        'exp_w_up': nrm((DEPTH, N_EXPERTS, D_MODEL, D_FF), D_MODEL ** -0.5),
        'exp_b_up': nrm((DEPTH, N_EXPERTS, D_FF), 0.01),
        'exp_w_down': nrm((DEPTH, N_EXPERTS, D_FF, D_MODEL), 0.5 * D_FF ** -0.5),
        'exp_b_down': nrm((DEPTH, N_EXPERTS, D_MODEL), 0.01),
    }


def reference(x_prompt, x_sample, cache_kv_cmp, cache_kv_slc, cache_kv_win, state_gdn_S, state_gdn_conv,
              state_mlstm_C, state_mlstm_n, state_mlstm_m, page_table,
              norm_mix, norm_ffn, norm_final, w_in, gdn_conv_w, gdn_a_log, gdn_dt_bias, gdn_norm_w,
              nsa_cmp_pe, nsa_cmp_w, rel_bias_table, mlstm_b_if, mlstm_norm_w,
              w_up_a, w_up_b, w_up_c, w_out, router_w, router_b,
              exp_w_gate, exp_b_gate, exp_w_up, exp_b_up, exp_w_down, exp_b_down):
    prm = {'w_in': w_in, 'gdn_conv_w': gdn_conv_w, 'gdn_a_log': gdn_a_log, 'gdn_dt_bias': gdn_dt_bias,
           'gdn_norm_w': gdn_norm_w, 'nsa_cmp_pe': nsa_cmp_pe, 'nsa_cmp_w': nsa_cmp_w,
           'rel_bias_table': rel_bias_table, 'mlstm_b_if': mlstm_b_if, 'mlstm_norm_w': mlstm_norm_w,
           'w_up_a': w_up_a, 'w_up_b': w_up_b, 'w_up_c': w_up_c, 'w_out': w_out}
    past = {'kv_cmp': cache_kv_cmp, 'kv_slc': cache_kv_slc, 'kv_win': cache_kv_win, 'gdn_S': state_gdn_S,
            'gdn_conv': state_gdn_conv, 'mlstm_C': state_mlstm_C, 'mlstm_n': state_mlstm_n,
            'mlstm_m': state_mlstm_m, 'page_table': page_table}

    def trunk(x, qpos, past_state):
        layer_states = []
        for l in range(DEPTH):
            mix, st = token_mixers(rms_norm(x, norm_mix[l]), l, qpos, prm, past_state)
            x = x + mix
            x = x + moe_ffn(rms_norm(x, norm_ffn[l]), l, router_w, router_b, exp_w_gate, exp_b_gate,
                            exp_w_up, exp_b_up, exp_w_down, exp_b_down)
            layer_states.append(st)
        stacked = [jnp.stack(s, axis=0) for s in zip(*layer_states)]
        return rms_norm(x, norm_final), stacked

    y_prompt, sp = trunk(x_prompt, jnp.arange(SEQ, dtype=jnp.int32), None)
    y_sample, ss = trunk(x_sample, PAST_LEN + jnp.arange(DEC_SEQ, dtype=jnp.int32), past)
    kvc_p, kvs_p, win_p, gS_p, gconv_p, mC_p, mn_p, mm_p = sp
    kvc_s, kvs_s, win_s, gS_s, gconv_s, mC_s, mn_s, mm_s = ss
    return (y_prompt, y_sample, kvc_p, kvc_s, kvs_p, kvs_s, win_p, win_s, gS_p, gS_s,
            gconv_p, gconv_s, mC_p, mC_s, mn_p, mn_s, mm_p, mm_s)
```

```python
import functools
import math

import jax
import jax.numpy as jnp
import numpy as np
from jax import lax
from jax.experimental import pallas as pl
from jax.experimental.pallas import tpu as pltpu

D_MODEL = 4096
BATCH = 4
SEQ = 2048
DEPTH = 2
DEC_BATCH = 8
DEC_SEQ = 4
PAST_LEN = 16384
PAGE_SIZE = 128

H_A = D_MODEL // 512
DK_A = 128
DV_A = 128
W_A = H_A * DV_A
CONV_W = 4
GDN_CHUNK = 64
H_B = D_MODEL // 256
KV_B = 4
GRP_B = H_B // KV_B
DH_B = 128
W_B = H_B * DH_B
CMP_BLK = 32
SLC_BLK = 64
N_SEL = 16
WINDOW = 512
WIN_QBLK = 128
SLC_QBLK = 16
SEL_FORCE = 1e9
H_C = D_MODEL // 512
DK_C = 64
DV_C = 128
W_C = H_C * DV_C
MLSTM_CHUNK = 64
GATE_CAP = 15.0
REL_BUCKETS = 32
REL_EXACT = REL_BUCKETS // 2
REL_MAX_DIST = 128
N_EXPERTS = 32
TOP_K = 4
D_FF = D_MODEL // 2
SWIGLU_ALPHA = 1.702
SWIGLU_LIMIT = 7.0
MOE_BLOCK = 128
MOE_BLOCK_SMALL = 8

RMS_EPS = 1e-6
NEG = -1e30

IN_SIZES = (3 * W_A, W_A, H_A, H_A,
            W_B, 6 * KV_B * DH_B, 3 * H_B,
            H_C * DK_C, H_C * DK_C, W_C, 2 * H_C, W_C,
            3 * D_MODEL)
IN_OFFSETS = tuple(int(c) for c in np.cumsum(IN_SIZES)[:-1])
IN_COLS = int(sum(IN_SIZES))

V7X_VMEM_LIMIT_BYTES = 48 * 1024 * 1024


def _mm_kernel(x_ref, w_ref, o_ref, acc_ref):
    k = pl.program_id(2)

    @pl.when(k == 0)
    def _():
        acc_ref[...] = jnp.zeros_like(acc_ref)

    acc_ref[...] += jnp.dot(x_ref[...].astype(jnp.bfloat16), w_ref[...].astype(jnp.bfloat16),
                            preferred_element_type=jnp.float32)

    @pl.when(k == pl.num_programs(2) - 1)
    def _():
        o_ref[...] = acc_ref[...]


def _matmul(x, w, tm=1024, tn=1024, tk=512):
    M, K = x.shape
    N = w.shape[1]
    tm = min(tm, M)
    tn = min(tn, N)
    tk = min(tk, K)
    assert M % tm == 0 and K % tk == 0
    return pl.pallas_call(
        _mm_kernel,
        out_shape=jax.ShapeDtypeStruct((M, N), jnp.float32),
        grid=(M // tm, pl.cdiv(N, tn), K // tk),
        in_specs=[pl.BlockSpec((tm, tk), lambda i, j, k: (i, k)),
                  pl.BlockSpec((tk, tn), lambda i, j, k: (k, j))],
        out_specs=pl.BlockSpec((tm, tn), lambda i, j, k: (i, j)),
        scratch_shapes=[pltpu.VMEM((tm, tn), jnp.float32)],
        compiler_params=pltpu.CompilerParams(
            dimension_semantics=("parallel", "parallel", "arbitrary"),
            vmem_limit_bytes=V7X_VMEM_LIMIT_BYTES),
        name="dense_matmul",
    )(x, w)


def _dot2(x, w):
    lead = x.shape[:-1]
    return _matmul(x.reshape(-1, x.shape[-1]), w).reshape(lead + (w.shape[1],))


def rms_norm(x, g):
    xf = x.astype(jnp.float32)
    y = xf * lax.rsqrt(jnp.mean(xf * xf, axis=-1, keepdims=True) + RMS_EPS)
    return (y * g.astype(jnp.float32)).astype(x.dtype)


def head_rms(x, w):
    return x * lax.rsqrt(jnp.mean(x * x, axis=-1, keepdims=True) + RMS_EPS) * w.astype(jnp.float32)


def l2_normalize(x):
    return x * lax.rsqrt(jnp.sum(x * x, axis=-1, keepdims=True) + RMS_EPS)


def softcap(x, cap):
    return cap * jnp.tanh(x / cap)


def masked_softmax(s, mask):
    p = jax.nn.softmax(jnp.where(mask, s, NEG), axis=-1)
    return jnp.where(mask, p, 0.0)


def rel_bucket(dist):
    n = jnp.maximum(dist, 0)
    nf = jnp.maximum(n, 1).astype(jnp.float32)
    ratio = jnp.log(nf / REL_EXACT) / math.log(REL_MAX_DIST / REL_EXACT)
    large = REL_EXACT + (ratio * (REL_BUCKETS - REL_EXACT)).astype(jnp.int32)
    return jnp.where(n < REL_EXACT, n, jnp.minimum(large, REL_BUCKETS - 1))


def rel_bias_heads(dist, table):
    b = table[rel_bucket(dist)].astype(jnp.float32).reshape(dist.shape + (KV_B, GRP_B))
    nd = dist.ndim
    return jnp.transpose(b, (nd, nd + 1) + tuple(range(nd)))


def pad_time(x, n_pad, value=0.0):
    if n_pad == 0:
        return x
    widths = [(0, 0)] * x.ndim
    widths[1] = (0, n_pad)
    return jnp.pad(x, widths, constant_values=value)


def to_chunks(x, chunk):
    B, T = x.shape[:2]
    x = x.reshape((B, T // chunk, chunk) + x.shape[2:])
    return jnp.moveaxis(x, (1, 3), (0, 2))


def from_chunks(x):
    x = jnp.moveaxis(x, (0, 2), (1, 3))
    return x.reshape((x.shape[0], x.shape[1] * x.shape[2]) + x.shape[3:])


def gdn_recurrence(q, k, v, g, beta, S0):
    T = q.shape[1]
    L = GDN_CHUNK
    n_pad = (-T) % L
    q, k, v, g, beta = [to_chunks(pad_time(t, n_pad), L) for t in (q, k, v, g, beta)]
    incl = jnp.tril(jnp.ones((L, L), bool))
    strict = jnp.tril(jnp.ones((L, L), bool), -1)
    eye = jnp.eye(L, dtype=jnp.float32)

    def step(S, inp):
        qi, ki, vi, gi, bi = inp
        gcum = jnp.cumsum(gi, axis=-1)
        decay = jnp.exp(jnp.where(incl, gcum[..., :, None] - gcum[..., None, :], NEG))
        kb = ki * bi[..., None]
        lower = jnp.where(strict, jnp.einsum('bhid,bhjd->bhij', kb, ki) * decay, 0.0)
        tinv = lax.linalg.triangular_solve(eye + lower, jnp.broadcast_to(eye, lower.shape),
                                           left_side=True, lower=True)
        u = jnp.einsum('bhij,bhjd->bhid', tinv, vi * bi[..., None])
        w = jnp.einsum('bhij,bhjd->bhid', tinv, kb * jnp.exp(gcum)[..., None])
        v_new = u - jnp.einsum('bhid,bhde->bhie', w, S)
        attn = jnp.where(incl, jnp.einsum('bhid,bhjd->bhij', qi, ki) * decay, 0.0)
        o = (jnp.einsum('bhid,bhde->bhie', qi * jnp.exp(gcum)[..., None], S)
             + jnp.einsum('bhij,bhje->bhie', attn, v_new))
        g_last = gcum[..., -1]
        S = (S * jnp.exp(g_last)[..., None, None]
             + jnp.einsum('bhid,bhie->bhde', ki * jnp.exp(g_last[..., None] - gcum)[..., None], v_new))
        return S, o

    S, o = lax.scan(step, S0, (q, k, v, g, beta))
    return from_chunks(o)[:, :T], S


def gdn_mixer(qkv_raw, z, a, b, conv_prev, S0, conv_w, a_log, dt_bias, norm_w):
    B, T, _ = qkv_raw.shape
    xp = jnp.concatenate([conv_prev.astype(qkv_raw.dtype), qkv_raw], axis=1)
    conv = sum(xp[:, j:j + T] * conv_w[j] for j in range(CONV_W))
    qkv = jax.nn.silu(conv.astype(jnp.float32)).reshape(B, T, 3, H_A, DK_A)
    q = l2_normalize(qkv[:, :, 0]) * DK_A ** -0.5
    k = l2_normalize(qkv[:, :, 1])
    v = qkv[:, :, 2]
    beta = jax.nn.sigmoid(b.astype(jnp.float32))
    g = -jnp.exp(a_log.astype(jnp.float32)) * jax.nn.softplus(a.astype(jnp.float32) + dt_bias.astype(jnp.float32))
    o, S = gdn_recurrence(q, k, v, g, beta, S0.astype(jnp.float32))
    o = head_rms(o, norm_w) * jax.nn.silu(z.astype(jnp.float32).reshape(B, T, H_A, DV_A))
    return o.reshape(B, T, W_A).astype(qkv_raw.dtype), S, xp[:, T:]


def mlstm_recurrence(q, k, v, ig, lf, C0, n0, m0):
    T = q.shape[1]
    L = MLSTM_CHUNK
    n_pad = (-T) % L
    q, k, v, lf = [to_chunks(pad_time(t, n_pad), L) for t in (q, k, v, lf)]
    ig = to_chunks(pad_time(ig, n_pad, NEG), L)
    incl = jnp.tril(jnp.ones((L, L), bool))

    def step(carry, inp):
        C, n, m = carry
        qi, ki, vi, ii, fi = inp
        b = jnp.cumsum(fi, axis=-1)
        dmat = jnp.where(incl, b[..., :, None] - b[..., None, :] + ii[..., None, :], NEG)
        inter = b + m[..., None]
        m_tok = jnp.maximum(jnp.max(dmat, axis=-1), inter)
        s = jnp.einsum('bhid,bhjd->bhij', qi, ki) * jnp.exp(dmat - m_tok[..., None])
        w_inter = jnp.exp(inter - m_tok)
        num = w_inter[..., None] * jnp.einsum('bhid,bhde->bhie', qi, C) + jnp.einsum('bhij,bhje->bhie', s, vi)
        den = w_inter * jnp.einsum('bhid,bhd->bhi', qi, n) + jnp.sum(s, axis=-1)
        h = num / jnp.maximum(jnp.abs(den), jnp.exp(-m_tok))[..., None]
        b_last = b[..., -1]
        wk_log = b_last[..., None] - b + ii
        m_new = jnp.maximum(b_last + m, jnp.max(wk_log, axis=-1))
        wk = jnp.exp(wk_log - m_new[..., None])
        dec = jnp.exp(b_last + m - m_new)
        C = dec[..., None, None] * C + jnp.einsum('bhj,bhjd,bhje->bhde', wk, ki, vi)
        n = dec[..., None] * n + jnp.einsum('bhj,bhjd->bhd', wk, ki)
        return (C, n, m_new), h

    (C, n, m), h = lax.scan(step, (C0, n0, m0), (q, k, v, ig, lf))
    return from_chunks(h)[:, :T], C, n, m


def mlstm_mixer(q_raw, k_raw, v_raw, if_raw, o_raw, C0, n0, m0, b_if, norm_w):
    B, T, _ = q_raw.shape
    f32 = jnp.float32
    q = q_raw.astype(f32).reshape(B, T, H_C, DK_C) * DK_C ** -0.5
    k = k_raw.astype(f32).reshape(B, T, H_C, DK_C)
    v = v_raw.astype(f32).reshape(B, T, H_C, DV_C)
    pre = softcap(if_raw.astype(f32).reshape(B, T, 2, H_C) + b_if.astype(f32), GATE_CAP)
    ig = pre[:, :, 0]
    lf = jax.nn.log_sigmoid(pre[:, :, 1])
    h, C, n, m = mlstm_recurrence(q, k, v, ig, lf, C0.astype(f32), n0.astype(f32), m0.astype(f32))
    h = head_rms(h, norm_w).reshape(B, T, W_C) * jax.nn.sigmoid(o_raw.astype(f32))
    return h.astype(q_raw.dtype), C, n, m


def nsa_compressed(q, qpos, kv_full, pe, w, table):
    B, Lk = kv_full.shape[:2]
    nb = Lk // CMP_BLK
    blocks = kv_full[:, :nb * CMP_BLK].reshape(B, nb, CMP_BLK, 2, KV_B, DH_B)
    pooled = jnp.mean(blocks * jnp.transpose(pe, (1, 0, 2))[None, None, :, :, None, :], axis=2)
    summ = jnp.einsum('bncgd,cde->bncge', pooled, w)
    end_pos = (jnp.arange(nb) + 1) * CMP_BLK - 1
    dist = qpos[:, None] - end_pos[None, :]
    s = (jnp.einsum('bgrqd,bngd->bgrqn', q, summ[:, :, 0]).astype(jnp.float32) * DH_B ** -0.5
         + rel_bias_heads(dist, table))
    p = masked_softmax(s, dist >= 0)
    o = jnp.einsum('bgrqn,bngd->bgrqd', p.astype(summ.dtype), summ[:, :, 1])
    return o, p


def nsa_select(p_cmp, qpos, total_len):
    n_slc = -(-total_len // SLC_BLK)
    ratio = SLC_BLK // CMP_BLK
    imp = jnp.sum(p_cmp, axis=2)
    imp = jnp.pad(imp, ((0, 0), (0, 0), (0, 0), (0, n_slc * ratio - imp.shape[-1])))
    imp = imp.reshape(imp.shape[:-1] + (n_slc, ratio)).sum(-1)
    blk = jnp.arange(n_slc)[None, :]
    cur = (qpos // SLC_BLK)[:, None]
    valid = blk <= cur
    forced = (blk == 0) | (blk == cur) | (blk == cur - 1)
    score = jnp.where(valid & forced, SEL_FORCE, jnp.where(valid, imp, -SEL_FORCE))
    _, idx = lax.top_k(score, min(N_SEL, n_slc))
    return idx


def nsa_selected(q, qpos, idx, fetch, table):
    B, _, _, Tq, _ = q.shape
    n_sel = idx.shape[-1]
    qb = math.gcd(Tq, SLC_QBLK)
    nq = Tq // qb
    q_ch = jnp.moveaxis(q.reshape(B, KV_B, GRP_B, nq, qb, DH_B), 3, 0)
    pos_ch = qpos.reshape(nq, qb)
    idx_ch = jnp.moveaxis(idx.reshape(B, KV_B, nq, qb, n_sel), 2, 0)
    table_g = table.reshape(REL_BUCKETS, KV_B, GRP_B)
    gi = jnp.arange(KV_B)[None, :, None, None]
    offs = jnp.arange(SLC_BLK)

    def one_block(args):
        qc, pc, ic = args
        kpos = (ic[..., None] * SLC_BLK + offs).reshape(B, KV_B, qb, n_sel * SLC_BLK)
        kc, vc = fetch(kpos)
        dist = pc[None, None, :, None] - kpos
        bias = jnp.moveaxis(table_g[rel_bucket(dist), gi].astype(jnp.float32), -1, 2)
        s = jnp.einsum('bgrqd,bgqkd->bgrqk', qc, kc).astype(jnp.float32) * DH_B ** -0.5 + bias
        p = masked_softmax(s, (dist >= 0)[:, :, None])
        return jnp.einsum('bgrqk,bgqkd->bgrqd', p.astype(vc.dtype), vc)

    o = lax.map(one_block, (q_ch, pos_ch, idx_ch))
    return jnp.moveaxis(o, 0, 3).reshape(B, KV_B, GRP_B, Tq, DH_B)


def nsa_window(q, qpos, kb, kpos, table):
    dist = qpos[:, :, None] - kpos[:, None, :]
    mask = (dist >= 0) & (dist < WINDOW) & (kpos[:, None, :] >= 0)
    s = (jnp.einsum('bgrnqd,bnkgd->bgrnqk', q, kb[:, :, :, 0]).astype(jnp.float32) * DH_B ** -0.5
         + rel_bias_heads(dist, table))
    p = masked_softmax(s, mask)
    return jnp.einsum('bgrnqk,bnkgd->bgrnqd', p.astype(kb.dtype), kb[:, :, :, 1])


def nsa_mixer(q_raw, qpos, gate_raw, kv_cmp_full, fetch, win_blocks, win_kpos, cmp_pe, cmp_w, table):
    B, Tq, _ = q_raw.shape
    q = q_raw.reshape(B, Tq, KV_B, GRP_B, DH_B).transpose(0, 2, 3, 1, 4)
    o_cmp, p_cmp = nsa_compressed(q, qpos, kv_cmp_full, cmp_pe, cmp_w, table)
    idx = nsa_select(p_cmp, qpos, kv_cmp_full.shape[1])
    o_slc = nsa_selected(q, qpos, idx, fetch, table)
    nq = win_blocks.shape[1]
    o_win = nsa_window(q.reshape(B, KV_B, GRP_B, nq, Tq // nq, DH_B), qpos.reshape(nq, Tq // nq),
                       win_blocks, win_kpos, table).reshape(B, KV_B, GRP_B, Tq, DH_B)
    g = jax.nn.sigmoid(gate_raw.astype(jnp.float32).reshape(B, Tq, 3, KV_B, GRP_B)).transpose(2, 0, 3, 4, 1)
    o = g[0][..., None] * o_cmp + g[1][..., None] * o_slc + g[2][..., None] * o_win
    return o.transpose(0, 3, 1, 2, 4).reshape(B, Tq, W_B).astype(q_raw.dtype)


def contiguous_fetch(kv):
    bi = jnp.arange(kv.shape[0])[:, None, None, None]
    gi = jnp.arange(KV_B)[None, :, None, None]

    def fetch(kpos):
        rows = kv[bi, kpos, :, gi]
        return rows[..., 0, :], rows[..., 1, :]
    return fetch


def paged_fetch(pool, l, page_table, kv_new):
    B, Tn = kv_new.shape[:2]
    bi = jnp.arange(B)[:, None, None, None]
    gi = jnp.arange(KV_B)[None, :, None, None]
    li = jnp.full((1, 1, 1, 1), l, jnp.int32)

    def fetch(kpos):
        in_past = kpos < PAST_LEN
        pc = jnp.minimum(kpos, PAST_LEN - 1)
        page = page_table[bi, pc // PAGE_SIZE]
        past_rows = pool[li, page, pc % PAGE_SIZE, :, gi].astype(kv_new.dtype)
        new_rows = kv_new[bi, jnp.clip(kpos - PAST_LEN, 0, Tn - 1), :, gi]
        rows = jnp.where(in_past[..., None, None], past_rows, new_rows)
        return rows[..., 0, :], rows[..., 1, :]
    return fetch


def moe_ffn(x, l, router_w, router_b, w_gate, b_gate, w_up, b_up, w_down, b_down):
    shape = x.shape
    xt = x.reshape(-1, D_MODEL)
    T = xt.shape[0]
    logits = jnp.dot(xt, router_w[l]).astype(jnp.float32) + router_b[l].astype(jnp.float32)
    top_val, top_exp = lax.top_k(logits, TOP_K)
    weights = jax.nn.softmax(top_val, axis=-1)
    n = T * TOP_K
    blk = MOE_BLOCK if n >= MOE_BLOCK * N_EXPERTS else MOE_BLOCK_SMALL
    flat_e = top_exp.reshape(-1)
    order = jnp.argsort(flat_e)
    e_sorted = flat_e[order]
    counts = jnp.zeros((N_EXPERTS,), jnp.int32).at[flat_e].add(1)
    padded = (counts + blk - 1) // blk * blk
    start = jnp.cumsum(counts) - counts
    pstart = jnp.cumsum(padded) - padded
    dest = pstart[e_sorted] + jnp.arange(n, dtype=jnp.int32) - start[e_sorted]
    n_blocks = -(-(n + N_EXPERTS * (blk - 1)) // blk)
    slot_tok = jnp.full((n_blocks * blk,), T, jnp.int32).at[dest].set((order // TOP_K).astype(jnp.int32))
    slot_w = jnp.zeros((n_blocks * blk,), jnp.float32).at[dest].set(weights.reshape(-1)[order])
    blk_exp = jnp.minimum(jnp.searchsorted(jnp.cumsum(padded), jnp.arange(n_blocks) * blk, side='right'),
                          N_EXPERTS - 1)
    x_ext = jnp.concatenate([xt, jnp.zeros((1, D_MODEL), xt.dtype)], axis=0)

    def run_block(args):
        tok, e = args
        xb = x_ext[tok]
        gate = jnp.minimum(jnp.dot(xb, w_gate[l, e]) + b_gate[l, e], SWIGLU_LIMIT)
        up = jnp.clip(jnp.dot(xb, w_up[l, e]) + b_up[l, e], -SWIGLU_LIMIT, SWIGLU_LIMIT)
        hb = (up + 1.0) * gate * jax.nn.sigmoid(SWIGLU_ALPHA * gate)
        return jnp.dot(hb, w_down[l, e]) + b_down[l, e]

    y_blk = lax.map(run_block, (slot_tok.reshape(n_blocks, blk), blk_exp))
    y = jnp.zeros((T + 1, D_MODEL), jnp.float32).at[slot_tok].add(
        y_blk.reshape(-1, D_MODEL).astype(jnp.float32) * slot_w[:, None])
    return y[:T].astype(x.dtype).reshape(shape)


def token_mixers(h, l, qpos, prm, past):
    B, T, _ = h.shape
    f32 = jnp.float32
    proj = _dot2(h, prm['w_in'][l])
    (a_qkv, a_z, a_a, a_b, b_q, b_kv, b_g, c_q, c_k, c_v, c_if, c_o, merge_g) = jnp.split(proj, IN_OFFSETS, axis=-1)
    if past is None:
        conv_prev = jnp.zeros((B, CONV_W - 1, 3 * W_A), h.dtype)
        S0 = jnp.zeros((B, H_A, DK_A, DV_A), f32)
        C0 = jnp.zeros((B, H_C, DK_C, DV_C), f32)
        n0 = jnp.zeros((B, H_C, DK_C), f32)
        m0 = jnp.full((B, H_C), NEG, f32)
    else:
        conv_prev = past['gdn_conv'][l]
        S0 = past['gdn_S'][l]
        C0 = past['mlstm_C'][l]
        n0 = past['mlstm_n'][l]
        m0 = past['mlstm_m'][l]

    o_a, S_a, conv_a = gdn_mixer(a_qkv, a_z, a_a, a_b, conv_prev, S0, prm['gdn_conv_w'][l],
                                 prm['gdn_a_log'][l], prm['gdn_dt_bias'][l], prm['gdn_norm_w'][l])

    kv_new = b_kv.reshape(B, T, 3, 2, KV_B, DH_B)
    kv_cmp, kv_slc, kv_win = kv_new[:, :, 0], kv_new[:, :, 1], kv_new[:, :, 2]
    if past is None:
        kv_cmp_full = kv_cmp
        fetch = contiguous_fetch(kv_slc)
        nq = T // WIN_QBLK
        span = WIN_QBLK + WINDOW
        kidx = jnp.arange(nq)[:, None] * WIN_QBLK + jnp.arange(span)[None, :]
        win_blocks = jnp.pad(kv_win, ((0, 0), (WINDOW, 0), (0, 0), (0, 0), (0, 0)))[:, kidx]
        win_kpos = kidx - WINDOW
        win_all = kv_win
    else:
        past_cmp = past['kv_cmp'][l, past['page_table']]
        kv_cmp_full = jnp.concatenate([past_cmp.reshape(B, -1, 2, KV_B, DH_B).astype(kv_cmp.dtype), kv_cmp], axis=1)
        fetch = paged_fetch(past['kv_slc'], l, past['page_table'], kv_slc)
        buf = past['kv_win'][l].astype(kv_win.dtype)
        wb = buf.shape[1]
        win_all = jnp.concatenate([buf, kv_win], axis=1)
        win_blocks = win_all[:, None]
        win_kpos = (PAST_LEN - wb + jnp.arange(wb + T))[None, :]
    o_b = nsa_mixer(b_q, qpos, b_g, kv_cmp_full, fetch, win_blocks, win_kpos,
                    prm['nsa_cmp_pe'][l], prm['nsa_cmp_w'][l], prm['rel_bias_table'])
    new_win = win_all[:, -min(WINDOW, win_all.shape[1]):]

    o_c, C, n, m = mlstm_mixer(c_q, c_k, c_v, c_if, c_o, C0, n0, m0, prm['mlstm_b_if'][l], prm['mlstm_norm_w'][l])

    g = jax.nn.sigmoid(merge_g.astype(f32).reshape(B, T, 3, D_MODEL))
    merged = (g[:, :, 0] * _dot2(o_a, prm['w_up_a'][l]) + g[:, :, 1] * _dot2(o_b, prm['w_up_b'][l])
              + g[:, :, 2] * _dot2(o_c, prm['w_up_c'][l]))
    out = _dot2(merged.astype(h.dtype), prm['w_out'][l])
    return out, (kv_cmp, kv_slc, new_win, S_a, conv_a, C, n, m)


def kernel(x_prompt, x_sample, cache_kv_cmp, cache_kv_slc, cache_kv_win, state_gdn_S, state_gdn_conv,
           state_mlstm_C, state_mlstm_n, state_mlstm_m, page_table,
           norm_mix, norm_ffn, norm_final, w_in, gdn_conv_w, gdn_a_log, gdn_dt_bias, gdn_norm_w,
           nsa_cmp_pe, nsa_cmp_w, rel_bias_table, mlstm_b_if, mlstm_norm_w,
           w_up_a, w_up_b, w_up_c, w_out, router_w, router_b,
           exp_w_gate, exp_b_gate, exp_w_up, exp_b_up, exp_w_down, exp_b_down):
    prm = {'w_in': w_in, 'gdn_conv_w': gdn_conv_w, 'gdn_a_log': gdn_a_log, 'gdn_dt_bias': gdn_dt_bias,
           'gdn_norm_w': gdn_norm_w, 'nsa_cmp_pe': nsa_cmp_pe, 'nsa_cmp_w': nsa_cmp_w,
           'rel_bias_table': rel_bias_table, 'mlstm_b_if': mlstm_b_if, 'mlstm_norm_w': mlstm_norm_w,
           'w_up_a': w_up_a, 'w_up_b': w_up_b, 'w_up_c': w_up_c, 'w_out': w_out}
    past = {'kv_cmp': cache_kv_cmp, 'kv_slc': cache_kv_slc, 'kv_win': cache_kv_win, 'gdn_S': state_gdn_S,
            'gdn_conv': state_gdn_conv, 'mlstm_C': state_mlstm_C, 'mlstm_n': state_mlstm_n,
            'mlstm_m': state_mlstm_m, 'page_table': page_table}

    def trunk(x, qpos, past_state):
        layer_states = []
        for l in range(DEPTH):
            mix, st = token_mixers(rms_norm(x, norm_mix[l]), l, qpos, prm, past_state)
            x = x + mix
            x = x + moe_ffn(rms_norm(x, norm_ffn[l]), l, router_w, router_b, exp_w_gate, exp_b_gate,
                            exp_w_up, exp_b_up, exp_w_down, exp_b_down)
            layer_states.append(st)
        stacked = [jnp.stack(s, axis=0) for s in zip(*layer_states)]
        return rms_norm(x, norm_final), stacked

    y_prompt, sp = trunk(x_prompt, jnp.arange(SEQ, dtype=jnp.int32), None)
    y_sample, ss = trunk(x_sample, PAST_LEN + jnp.arange(DEC_SEQ, dtype=jnp.int32), past)
    kvc_p, kvs_p, win_p, gS_p, gconv_p, mC_p, mn_p, mm_p = sp
    kvc_s, kvs_s, win_s, gS_s, gconv_s, mC_s, mn_s, mm_s = ss
    return (y_prompt, y_sample, kvc_p, kvc_s, kvs_p, kvs_s, win_p, win_s, gS_p, gS_s,
            gconv_p, gconv_s, mC_p, mC_s, mn_p, mn_s, mm_p, mm_s)
```

```python
import functools
import math

import jax
import jax.numpy as jnp
import numpy as np
from jax import lax
from jax.experimental import pallas as pl
from jax.experimental.pallas import tpu as pltpu

D_MODEL = 4096
BATCH = 4
SEQ = 2048
DEPTH = 2
DEC_BATCH = 8
DEC_SEQ = 4
PAST_LEN = 16384
PAGE_SIZE = 128

H_A = D_MODEL // 512
DK_A = 128
DV_A = 128
W_A = H_A * DV_A
CONV_W = 4
GDN_CHUNK = 64
H_B = D_MODEL // 256
KV_B = 4
GRP_B = H_B // KV_B
DH_B = 128
W_B = H_B * DH_B
CMP_BLK = 32
SLC_BLK = 64
N_SEL = 16
WINDOW = 512
WIN_QBLK = 128
SLC_QBLK = 16
SEL_FORCE = 1e9
H_C = D_MODEL // 512
DK_C = 64
DV_C = 128
W_C = H_C * DV_C
MLSTM_CHUNK = 64
GATE_CAP = 15.0
REL_BUCKETS = 32
REL_EXACT = REL_BUCKETS // 2
REL_MAX_DIST = 128
N_EXPERTS = 32
TOP_K = 4
D_FF = D_MODEL // 2
SWIGLU_ALPHA = 1.702
SWIGLU_LIMIT = 7.0
MOE_BLOCK = 128
MOE_BLOCK_SMALL = 8

RMS_EPS = 1e-6
NEG = -1e30

IN_SIZES = (3 * W_A, W_A, H_A, H_A,
            W_B, 6 * KV_B * DH_B, 3 * H_B,
            H_C * DK_C, H_C * DK_C, W_C, 2 * H_C, W_C,
            3 * D_MODEL)
IN_OFFSETS = tuple(int(c) for c in np.cumsum(IN_SIZES)[:-1])
IN_COLS = int(sum(IN_SIZES))

V7X_VMEM_LIMIT_BYTES = 48 * 1024 * 1024


def _mm_kernel(x_ref, w_ref, o_ref, acc_ref):
    k = pl.program_id(2)

    @pl.when(k == 0)
    def _():
        acc_ref[...] = jnp.zeros_like(acc_ref)

    acc_ref[...] += jnp.dot(x_ref[...].astype(jnp.bfloat16), w_ref[...].astype(jnp.bfloat16),
                            preferred_element_type=jnp.float32)

    @pl.when(k == pl.num_programs(2) - 1)
    def _():
        o_ref[...] = acc_ref[...]


def _matmul(x, w, tm=1024, tn=1024, tk=512):
    M, K = x.shape
    N = w.shape[1]
    tm = min(tm, M)
    tn = min(tn, N)
    tk = min(tk, K)
    assert M % tm == 0 and K % tk == 0
    return pl.pallas_call(
        _mm_kernel,
        out_shape=jax.ShapeDtypeStruct((M, N), jnp.float32),
        grid=(M // tm, pl.cdiv(N, tn), K // tk),
        in_specs=[pl.BlockSpec((tm, tk), lambda i, j, k: (i, k)),
                  pl.BlockSpec((tk, tn), lambda i, j, k: (k, j))],
        out_specs=pl.BlockSpec((tm, tn), lambda i, j, k: (i, j)),
        scratch_shapes=[pltpu.VMEM((tm, tn), jnp.float32)],
        compiler_params=pltpu.CompilerParams(
            dimension_semantics=("parallel", "parallel", "arbitrary"),
            vmem_limit_bytes=V7X_VMEM_LIMIT_BYTES),
        name="dense_matmul",
    )(x, w)


def _dot2(x, w):
    lead = x.shape[:-1]
    return _matmul(x.reshape(-1, x.shape[-1]), w).reshape(lead + (w.shape[1],))


def rms_norm(x, g):
    xf = x.astype(jnp.float32)
    y = xf * lax.rsqrt(jnp.mean(xf * xf, axis=-1, keepdims=True) + RMS_EPS)
    return (y * g.astype(jnp.float32)).astype(x.dtype)


def head_rms(x, w):
    return x * lax.rsqrt(jnp.mean(x * x, axis=-1, keepdims=True) + RMS_EPS) * w.astype(jnp.float32)


def l2_normalize(x):
    return x * lax.rsqrt(jnp.sum(x * x, axis=-1, keepdims=True) + RMS_EPS)


def softcap(x, cap):
    return cap * jnp.tanh(x / cap)


def masked_softmax(s, mask):
    p = jax.nn.softmax(jnp.where(mask, s, NEG), axis=-1)
    return jnp.where(mask, p, 0.0)


def rel_bucket(dist):
    n = jnp.maximum(dist, 0)
    nf = jnp.maximum(n, 1).astype(jnp.float32)
    ratio = jnp.log(nf / REL_EXACT) / math.log(REL_MAX_DIST / REL_EXACT)
    large = REL_EXACT + (ratio * (REL_BUCKETS - REL_EXACT)).astype(jnp.int32)
    return jnp.where(n < REL_EXACT, n, jnp.minimum(large, REL_BUCKETS - 1))


def rel_bias_heads(dist, table):
    b = table[rel_bucket(dist)].astype(jnp.float32).reshape(dist.shape + (KV_B, GRP_B))
    nd = dist.ndim
    return jnp.transpose(b, (nd, nd + 1) + tuple(range(nd)))


def pad_time(x, n_pad, value=0.0):
    if n_pad == 0:
        return x
    widths = [(0, 0)] * x.ndim
    widths[1] = (0, n_pad)
    return jnp.pad(x, widths, constant_values=value)


def to_chunks(x, chunk):
    B, T = x.shape[:2]
    x = x.reshape((B, T // chunk, chunk) + x.shape[2:])
    return jnp.moveaxis(x, (1, 3), (0, 2))


def from_chunks(x):
    x = jnp.moveaxis(x, (0, 2), (1, 3))
    return x.reshape((x.shape[0], x.shape[1] * x.shape[2]) + x.shape[3:])


def gdn_recurrence(q, k, v, g, beta, S0):
    T = q.shape[1]
    L = GDN_CHUNK
    n_pad = (-T) % L
    q, k, v, g, beta = [to_chunks(pad_time(t, n_pad), L) for t in (q, k, v, g, beta)]
    incl = jnp.tril(jnp.ones((L, L), bool))
    strict = jnp.tril(jnp.ones((L, L), bool), -1)
    eye = jnp.eye(L, dtype=jnp.float32)

    def step(S, inp):
        qi, ki, vi, gi, bi = inp
        gcum = jnp.cumsum(gi, axis=-1)
        decay = jnp.exp(jnp.where(incl, gcum[..., :, None] - gcum[..., None, :], NEG))
        kb = ki * bi[..., None]
        lower = jnp.where(strict, jnp.einsum('bhid,bhjd->bhij', kb, ki) * decay, 0.0)
        tinv = lax.linalg.triangular_solve(eye + lower, jnp.broadcast_to(eye, lower.shape),
                                           left_side=True, lower=True)
        u = jnp.einsum('bhij,bhjd->bhid', tinv, vi * bi[..., None])
        w = jnp.einsum('bhij,bhjd->bhid', tinv, kb * jnp.exp(gcum)[..., None])
        v_new = u - jnp.einsum('bhid,bhde->bhie', w, S)
        attn = jnp.where(incl, jnp.einsum('bhid,bhjd->bhij', qi, ki) * decay, 0.0)
        o = (jnp.einsum('bhid,bhde->bhie', qi * jnp.exp(gcum)[..., None], S)
             + jnp.einsum('bhij,bhje->bhie', attn, v_new))
        g_last = gcum[..., -1]
        S = (S * jnp.exp(g_last)[..., None, None]
             + jnp.einsum('bhid,bhie->bhde', ki * jnp.exp(g_last[..., None] - gcum)[..., None], v_new))
        return S, o

    S, o = lax.scan(step, S0, (q, k, v, g, beta))
    return from_chunks(o)[:, :T], S


def gdn_mixer(qkv_raw, z, a, b, conv_prev, S0, conv_w, a_log, dt_bias, norm_w):
    B, T, _ = qkv_raw.shape
    xp = jnp.concatenate([conv_prev.astype(qkv_raw.dtype), qkv_raw], axis=1)
    conv = sum(xp[:, j:j + T] * conv_w[j] for j in range(CONV_W))
    qkv = jax.nn.silu(conv.astype(jnp.float32)).reshape(B, T, 3, H_A, DK_A)
    q = l2_normalize(qkv[:, :, 0]) * DK_A ** -0.5
    k = l2_normalize(qkv[:, :, 1])
    v = qkv[:, :, 2]
    beta = jax.nn.sigmoid(b.astype(jnp.float32))
    g = -jnp.exp(a_log.astype(jnp.float32)) * jax.nn.softplus(a.astype(jnp.float32) + dt_bias.astype(jnp.float32))
    o, S = gdn_recurrence(q, k, v, g, beta, S0.astype(jnp.float32))
    o = head_rms(o, norm_w) * jax.nn.silu(z.astype(jnp.float32).reshape(B, T, H_A, DV_A))
    return o.reshape(B, T, W_A).astype(qkv_raw.dtype), S, xp[:, T:]


def mlstm_recurrence(q, k, v, ig, lf, C0, n0, m0):
    T = q.shape[1]
    L = MLSTM_CHUNK
    n_pad = (-T) % L
    q, k, v, lf = [to_chunks(pad_time(t, n_pad), L) for t in (q, k, v, lf)]
    ig = to_chunks(pad_time(ig, n_pad, NEG), L)
    incl = jnp.tril(jnp.ones((L, L), bool))

    def step(carry, inp):
        C, n, m = carry
        qi, ki, vi, ii, fi = inp
        b = jnp.cumsum(fi, axis=-1)
        dmat = jnp.where(incl, b[..., :, None] - b[..., None, :] + ii[..., None, :], NEG)
        inter = b + m[..., None]
        m_tok = jnp.maximum(jnp.max(dmat, axis=-1), inter)
        s = jnp.einsum('bhid,bhjd->bhij', qi, ki) * jnp.exp(dmat - m_tok[..., None])
        w_inter = jnp.exp(inter - m_tok)
        num = w_inter[..., None] * jnp.einsum('bhid,bhde->bhie', qi, C) + jnp.einsum('bhij,bhje->bhie', s, vi)
        den = w_inter * jnp.einsum('bhid,bhd->bhi', qi, n) + jnp.sum(s, axis=-1)
        h = num / jnp.maximum(jnp.abs(den), jnp.exp(-m_tok))[..., None]
        b_last = b[..., -1]
        wk_log = b_last[..., None] - b + ii
        m_new = jnp.maximum(b_last + m, jnp.max(wk_log, axis=-1))
        wk = jnp.exp(wk_log - m_new[..., None])
        dec = jnp.exp(b_last + m - m_new)
        C = dec[..., None, None] * C + jnp.einsum('bhj,bhjd,bhje->bhde', wk, ki, vi)
        n = dec[..., None] * n + jnp.einsum('bhj,bhjd->bhd', wk, ki)
        return (C, n, m_new), h

    (C, n, m), h = lax.scan(step, (C0, n0, m0), (q, k, v, ig, lf))
    return from_chunks(h)[:, :T], C, n, m


def mlstm_mixer(q_raw, k_raw, v_raw, if_raw, o_raw, C0, n0, m0, b_if, norm_w):
    B, T, _ = q_raw.shape
    f32 = jnp.float32
    q = q_raw.astype(f32).reshape(B, T, H_C, DK_C) * DK_C ** -0.5
    k = k_raw.astype(f32).reshape(B, T, H_C, DK_C)
    v = v_raw.astype(f32).reshape(B, T, H_C, DV_C)
    pre = softcap(if_raw.astype(f32).reshape(B, T, 2, H_C) + b_if.astype(f32), GATE_CAP)
    ig = pre[:, :, 0]
    lf = jax.nn.log_sigmoid(pre[:, :, 1])
    h, C, n, m = mlstm_recurrence(q, k, v, ig, lf, C0.astype(f32), n0.astype(f32), m0.astype(f32))
    h = head_rms(h, norm_w).reshape(B, T, W_C) * jax.nn.sigmoid(o_raw.astype(f32))
    return h.astype(q_raw.dtype), C, n, m


def nsa_compressed(q, qpos, kv_full, pe, w, table):
    B, Lk = kv_full.shape[:2]
    nb = Lk // CMP_BLK
    blocks = kv_full[:, :nb * CMP_BLK].reshape(B, nb, CMP_BLK, 2, KV_B, DH_B)
    pooled = jnp.mean(blocks * jnp.transpose(pe, (1, 0, 2))[None, None, :, :, None, :], axis=2)
    summ = jnp.einsum('bncgd,cde->bncge', pooled, w)
    end_pos = (jnp.arange(nb) + 1) * CMP_BLK - 1
    dist = qpos[:, None] - end_pos[None, :]
    s = (jnp.einsum('bgrqd,bngd->bgrqn', q, summ[:, :, 0]).astype(jnp.float32) * DH_B ** -0.5
         + rel_bias_heads(dist, table))
    p = masked_softmax(s, dist >= 0)
    o = jnp.einsum('bgrqn,bngd->bgrqd', p.astype(summ.dtype), summ[:, :, 1])
    return o, p


def nsa_select(p_cmp, qpos, total_len):
    n_slc = -(-total_len // SLC_BLK)
    ratio = SLC_BLK // CMP_BLK
    imp = jnp.sum(p_cmp, axis=2)
    imp = jnp.pad(imp, ((0, 0), (0, 0), (0, 0), (0, n_slc * ratio - imp.shape[-1])))
    imp = imp.reshape(imp.shape[:-1] + (n_slc, ratio)).sum(-1)
    blk = jnp.arange(n_slc)[None, :]
    cur = (qpos // SLC_BLK)[:, None]
    valid = blk <= cur
    forced = (blk == 0) | (blk == cur) | (blk == cur - 1)
    score = jnp.where(valid & forced, SEL_FORCE, jnp.where(valid, imp, -SEL_FORCE))
    _, idx = lax.top_k(score, min(N_SEL, n_slc))
    return idx


def nsa_selected(q, qpos, idx, fetch, table):
    B, _, _, Tq, _ = q.shape
    n_sel = idx.shape[-1]
    qb = math.gcd(Tq, SLC_QBLK)
    nq = Tq // qb
    q_ch = jnp.moveaxis(q.reshape(B, KV_B, GRP_B, nq, qb, DH_B), 3, 0)
    pos_ch = qpos.reshape(nq, qb)
    idx_ch = jnp.moveaxis(idx.reshape(B, KV_B, nq, qb, n_sel), 2, 0)
    table_g = table.reshape(REL_BUCKETS, KV_B, GRP_B)
    gi = jnp.arange(KV_B)[None, :, None, None]
    offs = jnp.arange(SLC_BLK)

    def one_block(args):
        qc, pc, ic = args
        kpos = (ic[..., None] * SLC_BLK + offs).reshape(B, KV_B, qb, n_sel * SLC_BLK)
        kc, vc = fetch(kpos)
        dist = pc[None, None, :, None] - kpos
        bias = jnp.moveaxis(table_g[rel_bucket(dist), gi].astype(jnp.float32), -1, 2)
        s = jnp.einsum('bgrqd,bgqkd->bgrqk', qc, kc).astype(jnp.float32) * DH_B ** -0.5 + bias
        p = masked_softmax(s, (dist >= 0)[:, :, None])
        return jnp.einsum('bgrqk,bgqkd->bgrqd', p.astype(vc.dtype), vc)

    o = lax.map(one_block, (q_ch, pos_ch, idx_ch))
    return jnp.moveaxis(o, 0, 3).reshape(B, KV_B, GRP_B, Tq, DH_B)


NSA_TILE = 256
SLC_SHIFT = SLC_BLK.bit_length() - 1
assert 1 << SLC_SHIFT == SLC_BLK


def _nsa_slc_kernel(cfar_ref, q_ref, k_ref, v_ref, sel_ref, band_ref, o_ref, m_ref, l_ref, acc_ref, *, n_slc):
    g = pl.program_id(1)
    qi = pl.program_id(2)
    tq = tk = NSA_TILE
    bf16 = jnp.bfloat16
    f32 = jnp.float32
    qs = [q_ref[0, :, r * DH_B:(r + 1) * DH_B].astype(bf16) for r in range(GRP_B)]
    selt = sel_ref[0, 0].astype(bf16)

    m_ref[...] = jnp.full(m_ref.shape, NEG, f32)
    l_ref[...] = jnp.zeros(l_ref.shape, f32)
    acc_ref[...] = jnp.zeros(acc_ref.shape, f32)

    def process(j, bias_of_head, causal):
        k0 = pl.multiple_of(j * tk, tk)
        kt = k_ref[0, pl.ds(k0, tk), :].astype(bf16)
        vt = v_ref[0, pl.ds(k0, tk), :].astype(bf16)
        key_blk = jnp.right_shift(k0 + lax.broadcasted_iota(jnp.int32, (n_slc, tk), 1), SLC_SHIFT)
        expand = (key_blk == lax.broadcasted_iota(jnp.int32, (n_slc, tk), 0)).astype(bf16)
        mask = jnp.dot(selt, expand, preferred_element_type=f32) > 0.5
        if causal:
            mask = jnp.logical_and(mask, lax.broadcasted_iota(jnp.int32, (tq, tk), 0)
                                   >= lax.broadcasted_iota(jnp.int32, (tq, tk), 1))
        for r in range(GRP_B):
            s = lax.dot_general(qs[r], kt, (((1,), (1,)), ((), ())), preferred_element_type=f32)
            s = jnp.where(mask, s * DH_B ** -0.5 + bias_of_head(r), NEG)
            m_prev = m_ref[r]
            m_next = jnp.maximum(m_prev, jnp.max(s, axis=-1, keepdims=True))
            p = jnp.where(mask, jnp.exp(s - m_next), 0.0)
            alpha = jnp.exp(m_prev - m_next)
            l_ref[r] = alpha * l_ref[r] + jnp.sum(p, axis=-1, keepdims=True)
            acc_ref[r] = alpha * acc_ref[r] + jnp.dot(p.astype(bf16), vt, preferred_element_type=f32)
            m_ref[r] = m_next

    def far_tile(j, carry):
        process(j, lambda r: cfar_ref[g * GRP_B + r], False)
        return carry

    lax.fori_loop(0, jnp.maximum(qi - 1, 0), far_tile, 0)

    @pl.when(qi >= 1)
    def _():
        process(qi - 1, lambda r: band_ref[1, 0, r], False)

    process(qi, lambda r: band_ref[0, 0, r], True)

    for r in range(GRP_B):
        o_ref[0, :, r * DH_B:(r + 1) * DH_B] = acc_ref[r] / l_ref[r]


def nsa_selected_contiguous(b_q, b_kv, idx, table):
    B, T, _ = b_q.shape
    tq = tk = NSA_TILE
    assert T % tq == 0 and tk >= REL_MAX_DIST and tk % SLC_BLK == 0
    n_slc = -(-T // SLC_BLK)
    sel = jnp.any(idx[..., None] == jnp.arange(n_slc, dtype=idx.dtype), axis=-2).astype(jnp.float32)
    rel = jnp.arange(tq)[:, None] - jnp.arange(tk)[None, :]
    dist = jnp.stack([rel, rel + tk])
    band = table[rel_bucket(dist)].astype(jnp.float32).reshape(2, tq, tk, KV_B, GRP_B).transpose(0, 3, 4, 1, 2)
    cfar = table[rel_bucket(jnp.int32(2 * tk))].astype(jnp.float32)
    k_col = (1 * 2 + 0) * KV_B
    v_col = (1 * 2 + 1) * KV_B
    grp_w = GRP_B * DH_B
    return pl.pallas_call(
        functools.partial(_nsa_slc_kernel, n_slc=n_slc),
        out_shape=jax.ShapeDtypeStruct((B, T, W_B), jnp.float32),
        grid=(B, KV_B, T // tq),
        in_specs=[pl.BlockSpec(memory_space=pltpu.SMEM),
                  pl.BlockSpec((1, tq, grp_w), lambda b, g, i: (b, i, g)),
                  pl.BlockSpec((1, T, DH_B), lambda b, g, i: (b, 0, k_col + g)),
                  pl.BlockSpec((1, T, DH_B), lambda b, g, i: (b, 0, v_col + g)),
                  pl.BlockSpec((1, 1, tq, n_slc), lambda b, g, i: (b, g, i, 0)),
                  pl.BlockSpec((2, 1, GRP_B, tq, tk), lambda b, g, i: (0, g, 0, 0, 0))],
        out_specs=pl.BlockSpec((1, tq, grp_w), lambda b, g, i: (b, i, g)),
        scratch_shapes=[pltpu.VMEM((GRP_B, tq, 1), jnp.float32),
                        pltpu.VMEM((GRP_B, tq, 1), jnp.float32),
                        pltpu.VMEM((GRP_B, tq, DH_B), jnp.float32)],
        compiler_params=pltpu.CompilerParams(
            dimension_semantics=("parallel", "parallel", "arbitrary"),
            vmem_limit_bytes=V7X_VMEM_LIMIT_BYTES),
        name="nsa_selected",
    )(cfar, b_q, b_kv, b_kv, sel, band)


def nsa_window(q, qpos, kb, kpos, table):
    dist = qpos[:, :, None] - kpos[:, None, :]
    mask = (dist >= 0) & (dist < WINDOW) & (kpos[:, None, :] >= 0)
    s = (jnp.einsum('bgrnqd,bnkgd->bgrnqk', q, kb[:, :, :, 0]).astype(jnp.float32) * DH_B ** -0.5
         + rel_bias_heads(dist, table))
    p = masked_softmax(s, mask)
    return jnp.einsum('bgrnqk,bnkgd->bgrnqd', p.astype(kb.dtype), kb[:, :, :, 1])


def nsa_mixer(q_raw, qpos, gate_raw, kv_cmp_full, fetch, win_blocks, win_kpos, cmp_pe, cmp_w, table,
              contiguous_kv=None):
    B, Tq, _ = q_raw.shape
    q = q_raw.reshape(B, Tq, KV_B, GRP_B, DH_B).transpose(0, 2, 3, 1, 4)
    o_cmp, p_cmp = nsa_compressed(q, qpos, kv_cmp_full, cmp_pe, cmp_w, table)
    idx = nsa_select(p_cmp, qpos, kv_cmp_full.shape[1])
    if contiguous_kv is None:
        o_slc = nsa_selected(q, qpos, idx, fetch, table)
    else:
        o_slc = nsa_selected_contiguous(q_raw, contiguous_kv, idx, table)
        o_slc = o_slc.reshape(B, Tq, KV_B, GRP_B, DH_B).transpose(0, 2, 3, 1, 4)
    nq = win_blocks.shape[1]
    o_win = nsa_window(q.reshape(B, KV_B, GRP_B, nq, Tq // nq, DH_B), qpos.reshape(nq, Tq // nq),
                       win_blocks, win_kpos, table).reshape(B, KV_B, GRP_B, Tq, DH_B)
    g = jax.nn.sigmoid(gate_raw.astype(jnp.float32).reshape(B, Tq, 3, KV_B, GRP_B)).transpose(2, 0, 3, 4, 1)
    o = g[0][..., None] * o_cmp + g[1][..., None] * o_slc + g[2][..., None] * o_win
    return o.transpose(0, 3, 1, 2, 4).reshape(B, Tq, W_B).astype(q_raw.dtype)


def contiguous_fetch(kv):
    bi = jnp.arange(kv.shape[0])[:, None, None, None]
    gi = jnp.arange(KV_B)[None, :, None, None]

    def fetch(kpos):
        rows = kv[bi, kpos, :, gi]
        return rows[..., 0, :], rows[..., 1, :]
    return fetch


def paged_fetch(pool, l, page_table, kv_new):
    B, Tn = kv_new.shape[:2]
    bi = jnp.arange(B)[:, None, None, None]
    gi = jnp.arange(KV_B)[None, :, None, None]
    li = jnp.full((1, 1, 1, 1), l, jnp.int32)

    def fetch(kpos):
        in_past = kpos < PAST_LEN
        pc = jnp.minimum(kpos, PAST_LEN - 1)
        page = page_table[bi, pc // PAGE_SIZE]
        past_rows = pool[li, page, pc % PAGE_SIZE, :, gi].astype(kv_new.dtype)
        new_rows = kv_new[bi, jnp.clip(kpos - PAST_LEN, 0, Tn - 1), :, gi]
        rows = jnp.where(in_past[..., None, None], past_rows, new_rows)
        return rows[..., 0, :], rows[..., 1, :]
    return fetch


def moe_ffn(x, l, router_w, router_b, w_gate, b_gate, w_up, b_up, w_down, b_down):
    shape = x.shape
    xt = x.reshape(-1, D_MODEL)
    T = xt.shape[0]
    logits = jnp.dot(xt, router_w[l]).astype(jnp.float32) + router_b[l].astype(jnp.float32)
    top_val, top_exp = lax.top_k(logits, TOP_K)
    weights = jax.nn.softmax(top_val, axis=-1)
    n = T * TOP_K
    blk = MOE_BLOCK if n >= MOE_BLOCK * N_EXPERTS else MOE_BLOCK_SMALL
    flat_e = top_exp.reshape(-1)
    order = jnp.argsort(flat_e)
    e_sorted = flat_e[order]
    counts = jnp.zeros((N_EXPERTS,), jnp.int32).at[flat_e].add(1)
    padded = (counts + blk - 1) // blk * blk
    start = jnp.cumsum(counts) - counts
    pstart = jnp.cumsum(padded) - padded
    dest = pstart[e_sorted] + jnp.arange(n, dtype=jnp.int32) - start[e_sorted]
    n_blocks = -(-(n + N_EXPERTS * (blk - 1)) // blk)
    slot_tok = jnp.full((n_blocks * blk,), T, jnp.int32).at[dest].set((order // TOP_K).astype(jnp.int32))
    slot_w = jnp.zeros((n_blocks * blk,), jnp.float32).at[dest].set(weights.reshape(-1)[order])
    blk_exp = jnp.minimum(jnp.searchsorted(jnp.cumsum(padded), jnp.arange(n_blocks) * blk, side='right'),
                          N_EXPERTS - 1)
    x_ext = jnp.concatenate([xt, jnp.zeros((1, D_MODEL), xt.dtype)], axis=0)

    def run_block(args):
        tok, e = args
        xb = x_ext[tok]
        gate = jnp.minimum(jnp.dot(xb, w_gate[l, e]) + b_gate[l, e], SWIGLU_LIMIT)
        up = jnp.clip(jnp.dot(xb, w_up[l, e]) + b_up[l, e], -SWIGLU_LIMIT, SWIGLU_LIMIT)
        hb = (up + 1.0) * gate * jax.nn.sigmoid(SWIGLU_ALPHA * gate)
        return jnp.dot(hb, w_down[l, e]) + b_down[l, e]

    y_blk = lax.map(run_block, (slot_tok.reshape(n_blocks, blk), blk_exp))
    y = jnp.zeros((T + 1, D_MODEL), jnp.float32).at[slot_tok].add(
        y_blk.reshape(-1, D_MODEL).astype(jnp.float32) * slot_w[:, None])
    return y[:T].astype(x.dtype).reshape(shape)


def token_mixers(h, l, qpos, prm, past):
    B, T, _ = h.shape
    f32 = jnp.float32
    proj = _dot2(h, prm['w_in'][l])
    (a_qkv, a_z, a_a, a_b, b_q, b_kv, b_g, c_q, c_k, c_v, c_if, c_o, merge_g) = jnp.split(proj, IN_OFFSETS, axis=-1)
    if past is None:
        conv_prev = jnp.zeros((B, CONV_W - 1, 3 * W_A), h.dtype)
        S0 = jnp.zeros((B, H_A, DK_A, DV_A), f32)
        C0 = jnp.zeros((B, H_C, DK_C, DV_C), f32)
        n0 = jnp.zeros((B, H_C, DK_C), f32)
        m0 = jnp.full((B, H_C), NEG, f32)
    else:
        conv_prev = past['gdn_conv'][l]
        S0 = past['gdn_S'][l]
        C0 = past['mlstm_C'][l]
        n0 = past['mlstm_n'][l]
        m0 = past['mlstm_m'][l]

    o_a, S_a, conv_a = gdn_mixer(a_qkv, a_z, a_a, a_b, conv_prev, S0, prm['gdn_conv_w'][l],
                                 prm['gdn_a_log'][l], prm['gdn_dt_bias'][l], prm['gdn_norm_w'][l])

    kv_new = b_kv.reshape(B, T, 3, 2, KV_B, DH_B)
    kv_cmp, kv_slc, kv_win = kv_new[:, :, 0], kv_new[:, :, 1], kv_new[:, :, 2]
    if past is None:
        kv_cmp_full = kv_cmp
        fetch = contiguous_fetch(kv_slc)
        nq = T // WIN_QBLK
        span = WIN_QBLK + WINDOW
        kidx = jnp.arange(nq)[:, None] * WIN_QBLK + jnp.arange(span)[None, :]
        win_blocks = jnp.pad(kv_win, ((0, 0), (WINDOW, 0), (0, 0), (0, 0), (0, 0)))[:, kidx]
        win_kpos = kidx - WINDOW
        win_all = kv_win
    else:
        past_cmp = past['kv_cmp'][l, past['page_table']]
        kv_cmp_full = jnp.concatenate([past_cmp.reshape(B, -1, 2, KV_B, DH_B).astype(kv_cmp.dtype), kv_cmp], axis=1)
        fetch = paged_fetch(past['kv_slc'], l, past['page_table'], kv_slc)
        buf = past['kv_win'][l].astype(kv_win.dtype)
        wb = buf.shape[1]
        win_all = jnp.concatenate([buf, kv_win], axis=1)
        win_blocks = win_all[:, None]
        win_kpos = (PAST_LEN - wb + jnp.arange(wb + T))[None, :]
    o_b = nsa_mixer(b_q, qpos, b_g, kv_cmp_full, fetch, win_blocks, win_kpos,
                    prm['nsa_cmp_pe'][l], prm['nsa_cmp_w'][l], prm['rel_bias_table'],
                    contiguous_kv=b_kv if past is None else None)
    new_win = win_all[:, -min(WINDOW, win_all.shape[1]):]

    o_c, C, n, m = mlstm_mixer(c_q, c_k, c_v, c_if, c_o, C0, n0, m0, prm['mlstm_b_if'][l], prm['mlstm_norm_w'][l])

    g = jax.nn.sigmoid(merge_g.astype(f32).reshape(B, T, 3, D_MODEL))
    merged = (g[:, :, 0] * _dot2(o_a, prm['w_up_a'][l]) + g[:, :, 1] * _dot2(o_b, prm['w_up_b'][l])
              + g[:, :, 2] * _dot2(o_c, prm['w_up_c'][l]))
    out = _dot2(merged.astype(h.dtype), prm['w_out'][l])
    return out, (kv_cmp, kv_slc, new_win, S_a, conv_a, C, n, m)


def kernel(x_prompt, x_sample, cache_kv_cmp, cache_kv_slc, cache_kv_win, state_gdn_S, state_gdn_conv,
           state_mlstm_C, state_mlstm_n, state_mlstm_m, page_table,
           norm_mix, norm_ffn, norm_final, w_in, gdn_conv_w, gdn_a_log, gdn_dt_bias, gdn_norm_w,
           nsa_cmp_pe, nsa_cmp_w, rel_bias_table, mlstm_b_if, mlstm_norm_w,
           w_up_a, w_up_b, w_up_c, w_out, router_w, router_b,
           exp_w_gate, exp_b_gate, exp_w_up, exp_b_up, exp_w_down, exp_b_down):
    prm = {'w_in': w_in, 'gdn_conv_w': gdn_conv_w, 'gdn_a_log': gdn_a_log, 'gdn_dt_bias': gdn_dt_bias,
           'gdn_norm_w': gdn_norm_w, 'nsa_cmp_pe': nsa_cmp_pe, 'nsa_cmp_w': nsa_cmp_w,
           'rel_bias_table': rel_bias_table, 'mlstm_b_if': mlstm_b_if, 'mlstm_norm_w': mlstm_norm_w,
           'w_up_a': w_up_a, 'w_up_b': w_up_b, 'w_up_c': w_up_c, 'w_out': w_out}
    past = {'kv_cmp': cache_kv_cmp, 'kv_slc': cache_kv_slc, 'kv_win': cache_kv_win, 'gdn_S': state_gdn_S,
            'gdn_conv': state_gdn_conv, 'mlstm_C': state_mlstm_C, 'mlstm_n': state_mlstm_n,
            'mlstm_m': state_mlstm_m, 'page_table': page_table}

    def trunk(x, qpos, past_state):
        layer_states = []
        for l in range(DEPTH):
            mix, st = token_mixers(rms_norm(x, norm_mix[l]), l, qpos, prm, past_state)
            x = x + mix
            x = x + moe_ffn(rms_norm(x, norm_ffn[l]), l, router_w, router_b, exp_w_gate, exp_b_gate,
                            exp_w_up, exp_b_up, exp_w_down, exp_b_down)
            layer_states.append(st)
        stacked = [jnp.stack(s, axis=0) for s in zip(*layer_states)]
        return rms_norm(x, norm_final), stacked

    y_prompt, sp = trunk(x_prompt, jnp.arange(SEQ, dtype=jnp.int32), None)
    y_sample, ss = trunk(x_sample, PAST_LEN + jnp.arange(DEC_SEQ, dtype=jnp.int32), past)
    kvc_p, kvs_p, win_p, gS_p, gconv_p, mC_p, mn_p, mm_p = sp
    kvc_s, kvs_s, win_s, gS_s, gconv_s, mC_s, mn_s, mm_s = ss
    return (y_prompt, y_sample, kvc_p, kvc_s, kvs_p, kvs_s, win_p, win_s, gS_p, gS_s,
            gconv_p, gconv_s, mC_p, mC_s, mn_p, mn_s, mm_p, mm_s)
```

```python
import functools
import math

import jax
import jax.numpy as jnp
import numpy as np
from jax import lax
from jax.experimental import pallas as pl
from jax.experimental.pallas import tpu as pltpu

D_MODEL = 4096
BATCH = 4
SEQ = 2048
DEPTH = 2
DEC_BATCH = 8
DEC_SEQ = 4
PAST_LEN = 16384
PAGE_SIZE = 128

H_A = D_MODEL // 512
DK_A = 128
DV_A = 128
W_A = H_A * DV_A
CONV_W = 4
GDN_CHUNK = 64
H_B = D_MODEL // 256
KV_B = 4
GRP_B = H_B // KV_B
DH_B = 128
W_B = H_B * DH_B
CMP_BLK = 32
SLC_BLK = 64
N_SEL = 16
WINDOW = 512
WIN_QBLK = 128
SLC_QBLK = 16
SEL_FORCE = 1e9
H_C = D_MODEL // 512
DK_C = 64
DV_C = 128
W_C = H_C * DV_C
MLSTM_CHUNK = 64
GATE_CAP = 15.0
REL_BUCKETS = 32
REL_EXACT = REL_BUCKETS // 2
REL_MAX_DIST = 128
N_EXPERTS = 32
TOP_K = 4
D_FF = D_MODEL // 2
SWIGLU_ALPHA = 1.702
SWIGLU_LIMIT = 7.0
MOE_BLOCK = 128
MOE_BLOCK_SMALL = 8

RMS_EPS = 1e-6
NEG = -1e30

IN_SIZES = (3 * W_A, W_A, H_A, H_A,
            W_B, 6 * KV_B * DH_B, 3 * H_B,
            H_C * DK_C, H_C * DK_C, W_C, 2 * H_C, W_C,
            3 * D_MODEL)
IN_OFFSETS = tuple(int(c) for c in np.cumsum(IN_SIZES)[:-1])
IN_COLS = int(sum(IN_SIZES))

V7X_VMEM_LIMIT_BYTES = 56 * 1024 * 1024


DENSE_ROWS = 1024
DENSE_COLS = 512


def _mm_kernel(x_ref, w_ref, o_ref, w_bf):
    @pl.when(pl.program_id(1) == 0)
    def _():
        w_bf[...] = w_ref[...].astype(jnp.bfloat16)

    o_ref[...] = jnp.dot(x_ref[...], w_bf[...], preferred_element_type=jnp.float32)


def _matmul(x, w):
    M, K = x.shape
    N = w.shape[1]
    tm = min(DENSE_ROWS, M)
    tn = min(DENSE_COLS, N)
    assert M % tm == 0
    return pl.pallas_call(
        _mm_kernel,
        out_shape=jax.ShapeDtypeStruct((M, N), jnp.float32),
        grid=(pl.cdiv(N, tn), M // tm),
        in_specs=[pl.BlockSpec((tm, K), lambda j, i: (i, 0)),
                  pl.BlockSpec((K, tn), lambda j, i: (0, j))],
        out_specs=pl.BlockSpec((tm, tn), lambda j, i: (i, j)),
        scratch_shapes=[pltpu.VMEM((K, tn), jnp.bfloat16)],
        compiler_params=pltpu.CompilerParams(
            dimension_semantics=("parallel", "arbitrary"),
            vmem_limit_bytes=V7X_VMEM_LIMIT_BYTES),
        name="dense_matmul",
    )(x, w)


def _dot2(x, w):
    lead = x.shape[:-1]
    return _matmul(x.reshape(-1, x.shape[-1]).astype(jnp.bfloat16), w).reshape(lead + (w.shape[1],))


def rms_norm(x, g):
    xf = x.astype(jnp.float32)
    y = xf * lax.rsqrt(jnp.mean(xf * xf, axis=-1, keepdims=True) + RMS_EPS)
    return (y * g.astype(jnp.float32)).astype(x.dtype)


def head_rms(x, w):
    return x * lax.rsqrt(jnp.mean(x * x, axis=-1, keepdims=True) + RMS_EPS) * w.astype(jnp.float32)


def l2_normalize(x):
    return x * lax.rsqrt(jnp.sum(x * x, axis=-1, keepdims=True) + RMS_EPS)


def softcap(x, cap):
    return cap * jnp.tanh(x / cap)


def masked_softmax(s, mask):
    p = jax.nn.softmax(jnp.where(mask, s, NEG), axis=-1)
    return jnp.where(mask, p, 0.0)


def rel_bucket(dist):
    n = jnp.maximum(dist, 0)
    nf = jnp.maximum(n, 1).astype(jnp.float32)
    ratio = jnp.log(nf / REL_EXACT) / math.log(REL_MAX_DIST / REL_EXACT)
    large = REL_EXACT + (ratio * (REL_BUCKETS - REL_EXACT)).astype(jnp.int32)
    return jnp.where(n < REL_EXACT, n, jnp.minimum(large, REL_BUCKETS - 1))


def rel_bias_heads(dist, table):
    b = table[rel_bucket(dist)].astype(jnp.float32).reshape(dist.shape + (KV_B, GRP_B))
    nd = dist.ndim
    return jnp.transpose(b, (nd, nd + 1) + tuple(range(nd)))


def pad_time(x, n_pad, value=0.0):
    if n_pad == 0:
        return x
    widths = [(0, 0)] * x.ndim
    widths[1] = (0, n_pad)
    return jnp.pad(x, widths, constant_values=value)


def to_chunks(x, chunk):
    B, T = x.shape[:2]
    x = x.reshape((B, T // chunk, chunk) + x.shape[2:])
    return jnp.moveaxis(x, (1, 3), (0, 2))


def from_chunks(x):
    x = jnp.moveaxis(x, (0, 2), (1, 3))
    return x.reshape((x.shape[0], x.shape[1] * x.shape[2]) + x.shape[3:])


GDN_HEADS_PER_STEP = 2


def _gdn_kernel(q_ref, k_ref, v_ref, gcol_ref, grow_ref, bcol_ref, s0_ref, o_ref, s_ref, *, n_chunks):
    L = GDN_CHUNK
    hp = GDN_HEADS_PER_STEP
    f32 = jnp.float32
    bf16 = jnp.bfloat16
    row = lax.broadcasted_iota(jnp.int32, (L, L), 0)
    col = lax.broadcasted_iota(jnp.int32, (L, L), 1)
    incl = row >= col
    strict = row > col
    eye = (row == col).astype(f32)

    def mm(a, b):
        return jnp.dot(a.astype(bf16), b.astype(bf16), preferred_element_type=f32)

    def mm_nt(a, b):
        return lax.dot_general(a.astype(bf16), b.astype(bf16), (((1,), (1,)), ((), ())), preferred_element_type=f32)

    def mm_exact(a, b):
        return jnp.dot(a, b, preferred_element_type=f32, precision=lax.Precision.HIGHEST)

    s_ref[...] = s0_ref[...]

    def chunk(c, carry):
        t0 = pl.multiple_of(c * L, L)
        for h in range(hp):
            lanes = slice(h * DK_A, (h + 1) * DK_A)
            q = q_ref[0, pl.ds(t0, L), lanes]
            k = k_ref[0, pl.ds(t0, L), lanes]
            v = v_ref[0, pl.ds(t0, L), lanes]
            gc = gcol_ref[0, h, pl.ds(t0, L), :]
            gr = grow_ref[0, h, pl.ds(c, 1), :]
            bc = bcol_ref[0, h, pl.ds(t0, L), :]
            S = s_ref[0, h]
            decay = jnp.exp(jnp.where(incl, gc - gr, NEG))
            kb = k * bc
            lower = jnp.where(strict, mm_nt(kb, k) * decay, 0.0)
            tinv = eye - lower
            power = mm_exact(lower, lower)
            n_levels = L.bit_length() - 2
            for level in range(n_levels):
                tinv = tinv + mm_exact(tinv, power)
                if level < n_levels - 1:
                    power = mm_exact(power, power)
            u = mm(tinv, v * bc)
            w = mm(tinv, kb * jnp.exp(gc))
            v_new = u - mm(w, S)
            attn = jnp.where(incl, mm_nt(q, k) * decay, 0.0)
            o_ref[0, pl.ds(t0, L), lanes] = mm(q * jnp.exp(gc), S) + mm(attn, v_new)
            g_last = gr[:, L - 1:L]
            k_dec = (k * jnp.exp(g_last - gc)).astype(bf16)
            s_ref[0, h] = S * jnp.exp(g_last) + lax.dot_general(
                k_dec, v_new.astype(bf16), (((0,), (0,)), ((), ())), preferred_element_type=f32)
        return carry

    lax.fori_loop(0, n_chunks, chunk, 0)


def gdn_recurrence(q, k, v, g, beta, S0):
    B, T, H, _ = q.shape
    L = GDN_CHUNK
    hp = GDN_HEADS_PER_STEP
    assert DK_A == DV_A and H % hp == 0 and L & (L - 1) == 0
    n_pad = (-T) % L
    q, k, v, g, beta = [pad_time(t, n_pad) for t in (q, k, v, g, beta)]
    Tp = T + n_pad
    nc = Tp // L
    gcum = jnp.cumsum(g.reshape(B, nc, L, H), axis=2).transpose(0, 3, 1, 2)
    gcol = gcum.reshape(B, H, Tp, 1)
    bcol = beta.transpose(0, 2, 1).reshape(B, H, Tp, 1)
    seq_spec = pl.BlockSpec((1, Tp, hp * DK_A), lambda b, h: (b, 0, h))
    col_spec = pl.BlockSpec((1, hp, Tp, 1), lambda b, h: (b, h, 0, 0))
    state_spec = pl.BlockSpec((1, hp, DK_A, DV_A), lambda b, h: (b, h, 0, 0))
    o, S = pl.pallas_call(
        functools.partial(_gdn_kernel, n_chunks=nc),
        out_shape=(jax.ShapeDtypeStruct((B, Tp, H * DV_A), jnp.float32),
                   jax.ShapeDtypeStruct((B, H, DK_A, DV_A), jnp.float32)),
        grid=(B, H // hp),
        in_specs=[seq_spec, seq_spec, seq_spec, col_spec,
                  pl.BlockSpec((1, hp, nc, L), lambda b, h: (b, h, 0, 0)), col_spec, state_spec],
        out_specs=(seq_spec, state_spec),
        compiler_params=pltpu.CompilerParams(dimension_semantics=("parallel", "parallel"),
                                             vmem_limit_bytes=V7X_VMEM_LIMIT_BYTES),
        name="gdn_chunks",
    )(q.reshape(B, Tp, H * DK_A), k.reshape(B, Tp, H * DK_A), v.reshape(B, Tp, H * DV_A), gcol, gcum, bcol, S0)
    return o.reshape(B, Tp, H, DV_A)[:, :T], S


def gdn_mixer(qkv_raw, z, a, b, conv_prev, S0, conv_w, a_log, dt_bias, norm_w):
    B, T, _ = qkv_raw.shape
    xp = jnp.concatenate([conv_prev.astype(qkv_raw.dtype), qkv_raw], axis=1)
    conv = sum(xp[:, j:j + T] * conv_w[j] for j in range(CONV_W))
    qkv = jax.nn.silu(conv.astype(jnp.float32)).reshape(B, T, 3, H_A, DK_A)
    q = l2_normalize(qkv[:, :, 0]) * DK_A ** -0.5
    k = l2_normalize(qkv[:, :, 1])
    v = qkv[:, :, 2]
    beta = jax.nn.sigmoid(b.astype(jnp.float32))
    g = -jnp.exp(a_log.astype(jnp.float32)) * jax.nn.softplus(a.astype(jnp.float32) + dt_bias.astype(jnp.float32))
    o, S = gdn_recurrence(q, k, v, g, beta, S0.astype(jnp.float32))
    o = head_rms(o, norm_w) * jax.nn.silu(z.astype(jnp.float32).reshape(B, T, H_A, DV_A))
    return o.reshape(B, T, W_A).astype(qkv_raw.dtype), S, xp[:, T:]


def mlstm_recurrence(q, k, v, ig, lf, C0, n0, m0):
    T = q.shape[1]
    L = MLSTM_CHUNK
    n_pad = (-T) % L
    q, k, v, lf = [to_chunks(pad_time(t, n_pad), L) for t in (q, k, v, lf)]
    ig = to_chunks(pad_time(ig, n_pad, NEG), L)
    incl = jnp.tril(jnp.ones((L, L), bool))

    def step(carry, inp):
        C, n, m = carry
        qi, ki, vi, ii, fi = inp
        b = jnp.cumsum(fi, axis=-1)
        dmat = jnp.where(incl, b[..., :, None] - b[..., None, :] + ii[..., None, :], NEG)
        inter = b + m[..., None]
        m_tok = jnp.maximum(jnp.max(dmat, axis=-1), inter)
        s = jnp.einsum('bhid,bhjd->bhij', qi, ki) * jnp.exp(dmat - m_tok[..., None])
        w_inter = jnp.exp(inter - m_tok)
        num = w_inter[..., None] * jnp.einsum('bhid,bhde->bhie', qi, C) + jnp.einsum('bhij,bhje->bhie', s, vi)
        den = w_inter * jnp.einsum('bhid,bhd->bhi', qi, n) + jnp.sum(s, axis=-1)
        h = num / jnp.maximum(jnp.abs(den), jnp.exp(-m_tok))[..., None]
        b_last = b[..., -1]
        wk_log = b_last[..., None] - b + ii
        m_new = jnp.maximum(b_last + m, jnp.max(wk_log, axis=-1))
        wk = jnp.exp(wk_log - m_new[..., None])
        dec = jnp.exp(b_last + m - m_new)
        C = dec[..., None, None] * C + jnp.einsum('bhj,bhjd,bhje->bhde', wk, ki, vi)
        n = dec[..., None] * n + jnp.einsum('bhj,bhjd->bhd', wk, ki)
        return (C, n, m_new), h

    (C, n, m), h = lax.scan(step, (C0, n0, m0), (q, k, v, ig, lf))
    return from_chunks(h)[:, :T], C, n, m


def mlstm_mixer(q_raw, k_raw, v_raw, if_raw, o_raw, C0, n0, m0, b_if, norm_w):
    B, T, _ = q_raw.shape
    f32 = jnp.float32
    q = q_raw.astype(f32).reshape(B, T, H_C, DK_C) * DK_C ** -0.5
    k = k_raw.astype(f32).reshape(B, T, H_C, DK_C)
    v = v_raw.astype(f32).reshape(B, T, H_C, DV_C)
    pre = softcap(if_raw.astype(f32).reshape(B, T, 2, H_C) + b_if.astype(f32), GATE_CAP)
    ig = pre[:, :, 0]
    lf = jax.nn.log_sigmoid(pre[:, :, 1])
    h, C, n, m = mlstm_recurrence(q, k, v, ig, lf, C0.astype(f32), n0.astype(f32), m0.astype(f32))
    h = head_rms(h, norm_w).reshape(B, T, W_C) * jax.nn.sigmoid(o_raw.astype(f32))
    return h.astype(q_raw.dtype), C, n, m


def nsa_compressed(q, qpos, kv_full, pe, w, table):
    B, Lk = kv_full.shape[:2]
    nb = Lk // CMP_BLK
    blocks = kv_full[:, :nb * CMP_BLK].reshape(B, nb, CMP_BLK, 2, KV_B, DH_B)
    pooled = jnp.mean(blocks * jnp.transpose(pe, (1, 0, 2))[None, None, :, :, None, :], axis=2)
    summ = jnp.einsum('bncgd,cde->bncge', pooled, w)
    end_pos = (jnp.arange(nb) + 1) * CMP_BLK - 1
    dist = qpos[:, None] - end_pos[None, :]
    s = (jnp.einsum('bgrqd,bngd->bgrqn', q, summ[:, :, 0]).astype(jnp.float32) * DH_B ** -0.5
         + rel_bias_heads(dist, table))
    p = masked_softmax(s, dist >= 0)
    o = jnp.einsum('bgrqn,bngd->bgrqd', p.astype(summ.dtype), summ[:, :, 1])
    return o, p


def nsa_select(p_cmp, qpos, total_len):
    n_slc = -(-total_len // SLC_BLK)
    ratio = SLC_BLK // CMP_BLK
    imp = jnp.sum(p_cmp, axis=2)
    imp = jnp.pad(imp, ((0, 0), (0, 0), (0, 0), (0, n_slc * ratio - imp.shape[-1])))
    imp = imp.reshape(imp.shape[:-1] + (n_slc, ratio)).sum(-1)
    blk = jnp.arange(n_slc)[None, :]
    cur = (qpos // SLC_BLK)[:, None]
    valid = blk <= cur
    forced = (blk == 0) | (blk == cur) | (blk == cur - 1)
    score = jnp.where(valid & forced, SEL_FORCE, jnp.where(valid, imp, -SEL_FORCE))
    _, idx = lax.top_k(score, min(N_SEL, n_slc))
    return idx


def nsa_selected(q, qpos, idx, fetch, table):
    B, _, _, Tq, _ = q.shape
    n_sel = idx.shape[-1]
    qb = math.gcd(Tq, SLC_QBLK)
    nq = Tq // qb
    q_ch = jnp.moveaxis(q.reshape(B, KV_B, GRP_B, nq, qb, DH_B), 3, 0)
    pos_ch = qpos.reshape(nq, qb)
    idx_ch = jnp.moveaxis(idx.reshape(B, KV_B, nq, qb, n_sel), 2, 0)
    table_g = table.reshape(REL_BUCKETS, KV_B, GRP_B)
    gi = jnp.arange(KV_B)[None, :, None, None]
    offs = jnp.arange(SLC_BLK)

    def one_block(args):
        qc, pc, ic = args
        kpos = (ic[..., None] * SLC_BLK + offs).reshape(B, KV_B, qb, n_sel * SLC_BLK)
        kc, vc = fetch(kpos)
        dist = pc[None, None, :, None] - kpos
        bias = jnp.moveaxis(table_g[rel_bucket(dist), gi].astype(jnp.float32), -1, 2)
        s = jnp.einsum('bgrqd,bgqkd->bgrqk', qc, kc).astype(jnp.float32) * DH_B ** -0.5 + bias
        p = masked_softmax(s, (dist >= 0)[:, :, None])
        return jnp.einsum('bgrqk,bgqkd->bgrqd', p.astype(vc.dtype), vc)

    o = lax.map(one_block, (q_ch, pos_ch, idx_ch))
    return jnp.moveaxis(o, 0, 3).reshape(B, KV_B, GRP_B, Tq, DH_B)


NSA_TILE = 256
SLC_SHIFT = SLC_BLK.bit_length() - 1
assert 1 << SLC_SHIFT == SLC_BLK


def _nsa_slc_kernel(cfar_ref, q_ref, k_ref, v_ref, sel_ref, band_ref, o_ref, m_ref, l_ref, acc_ref, *, n_slc):
    g = pl.program_id(1)
    qi = pl.program_id(2)
    tq = tk = NSA_TILE
    bf16 = jnp.bfloat16
    f32 = jnp.float32
    qs = [q_ref[0, :, r * DH_B:(r + 1) * DH_B].astype(bf16) for r in range(GRP_B)]
    selt = sel_ref[0, 0].astype(bf16)

    m_ref[...] = jnp.full(m_ref.shape, NEG, f32)
    l_ref[...] = jnp.zeros(l_ref.shape, f32)
    acc_ref[...] = jnp.zeros(acc_ref.shape, f32)

    def process(j, bias_of_head, causal):
        k0 = pl.multiple_of(j * tk, tk)
        kt = k_ref[0, pl.ds(k0, tk), :].astype(bf16)
        vt = v_ref[0, pl.ds(k0, tk), :].astype(bf16)
        key_blk = jnp.right_shift(k0 + lax.broadcasted_iota(jnp.int32, (n_slc, tk), 1), SLC_SHIFT)
        expand = (key_blk == lax.broadcasted_iota(jnp.int32, (n_slc, tk), 0)).astype(bf16)
        mask = jnp.dot(selt, expand, preferred_element_type=f32) > 0.5
        if causal:
            mask = jnp.logical_and(mask, lax.broadcasted_iota(jnp.int32, (tq, tk), 0)
                                   >= lax.broadcasted_iota(jnp.int32, (tq, tk), 1))
        for r in range(GRP_B):
            s = lax.dot_general(qs[r], kt, (((1,), (1,)), ((), ())), preferred_element_type=f32)
            s = jnp.where(mask, s * DH_B ** -0.5 + bias_of_head(r), NEG)
            m_prev = m_ref[r]
            m_next = jnp.maximum(m_prev, jnp.max(s, axis=-1, keepdims=True))
            p = jnp.where(mask, jnp.exp(s - m_next), 0.0)
            alpha = jnp.exp(m_prev - m_next)
            l_ref[r] = alpha * l_ref[r] + jnp.sum(p, axis=-1, keepdims=True)
            acc_ref[r] = alpha * acc_ref[r] + jnp.dot(p.astype(bf16), vt, preferred_element_type=f32)
            m_ref[r] = m_next

    def far_tile(j, carry):
        process(j, lambda r: cfar_ref[g * GRP_B + r], False)
        return carry

    lax.fori_loop(0, jnp.maximum(qi - 1, 0), far_tile, 0)

    @pl.when(qi >= 1)
    def _():
        process(qi - 1, lambda r: band_ref[1, 0, r], False)

    process(qi, lambda r: band_ref[0, 0, r], True)

    for r in range(GRP_B):
        o_ref[0, :, r * DH_B:(r + 1) * DH_B] = acc_ref[r] / l_ref[r]


def nsa_selected_contiguous(b_q, b_kv, idx, table):
    B, T, _ = b_q.shape
    tq = tk = NSA_TILE
    assert T % tq == 0 and tk >= REL_MAX_DIST and tk % SLC_BLK == 0
    n_slc = -(-T // SLC_BLK)
    sel = jnp.any(idx[..., None] == jnp.arange(n_slc, dtype=idx.dtype), axis=-2).astype(jnp.float32)
    rel = jnp.arange(tq)[:, None] - jnp.arange(tk)[None, :]
    dist = jnp.stack([rel, rel + tk])
    band = table[rel_bucket(dist)].astype(jnp.float32).reshape(2, tq, tk, KV_B, GRP_B).transpose(0, 3, 4, 1, 2)
    cfar = table[rel_bucket(jnp.int32(2 * tk))].astype(jnp.float32)
    k_col = (1 * 2 + 0) * KV_B
    v_col = (1 * 2 + 1) * KV_B
    grp_w = GRP_B * DH_B
    return pl.pallas_call(
        functools.partial(_nsa_slc_kernel, n_slc=n_slc),
        out_shape=jax.ShapeDtypeStruct((B, T, W_B), jnp.float32),
        grid=(B, KV_B, T // tq),
        in_specs=[pl.BlockSpec(memory_space=pltpu.SMEM),
                  pl.BlockSpec((1, tq, grp_w), lambda b, g, i: (b, i, g)),
                  pl.BlockSpec((1, T, DH_B), lambda b, g, i: (b, 0, k_col + g)),
                  pl.BlockSpec((1, T, DH_B), lambda b, g, i: (b, 0, v_col + g)),
                  pl.BlockSpec((1, 1, tq, n_slc), lambda b, g, i: (b, g, i, 0)),
                  pl.BlockSpec((2, 1, GRP_B, tq, tk), lambda b, g, i: (0, g, 0, 0, 0))],
        out_specs=pl.BlockSpec((1, tq, grp_w), lambda b, g, i: (b, i, g)),
        scratch_shapes=[pltpu.VMEM((GRP_B, tq, 1), jnp.float32),
                        pltpu.VMEM((GRP_B, tq, 1), jnp.float32),
                        pltpu.VMEM((GRP_B, tq, DH_B), jnp.float32)],
        compiler_params=pltpu.CompilerParams(
            dimension_semantics=("parallel", "parallel", "arbitrary"),
            vmem_limit_bytes=V7X_VMEM_LIMIT_BYTES),
        name="nsa_selected",
    )(cfar, b_q, b_kv, b_kv, sel, band)


def nsa_window(q, qpos, kb, kpos, table):
    dist = qpos[:, :, None] - kpos[:, None, :]
    mask = (dist >= 0) & (dist < WINDOW) & (kpos[:, None, :] >= 0)
    s = (jnp.einsum('bgrnqd,bnkgd->bgrnqk', q, kb[:, :, :, 0]).astype(jnp.float32) * DH_B ** -0.5
         + rel_bias_heads(dist, table))
    p = masked_softmax(s, mask)
    return jnp.einsum('bgrnqk,bnkgd->bgrnqd', p.astype(kb.dtype), kb[:, :, :, 1])


def nsa_mixer(q_raw, qpos, gate_raw, kv_cmp_full, fetch, win_blocks, win_kpos, cmp_pe, cmp_w, table,
              contiguous_kv=None):
    B, Tq, _ = q_raw.shape
    q = q_raw.reshape(B, Tq, KV_B, GRP_B, DH_B).transpose(0, 2, 3, 1, 4)
    o_cmp, p_cmp = nsa_compressed(q, qpos, kv_cmp_full, cmp_pe, cmp_w, table)
    idx = nsa_select(p_cmp, qpos, kv_cmp_full.shape[1])
    if contiguous_kv is None:
        o_slc = nsa_selected(q, qpos, idx, fetch, table)
    else:
        o_slc = nsa_selected_contiguous(q_raw, contiguous_kv, idx, table)
        o_slc = o_slc.reshape(B, Tq, KV_B, GRP_B, DH_B).transpose(0, 2, 3, 1, 4)
    nq = win_blocks.shape[1]
    o_win = nsa_window(q.reshape(B, KV_B, GRP_B, nq, Tq // nq, DH_B), qpos.reshape(nq, Tq // nq),
                       win_blocks, win_kpos, table).reshape(B, KV_B, GRP_B, Tq, DH_B)
    g = jax.nn.sigmoid(gate_raw.astype(jnp.float32).reshape(B, Tq, 3, KV_B, GRP_B)).transpose(2, 0, 3, 4, 1)
    o = g[0][..., None] * o_cmp + g[1][..., None] * o_slc + g[2][..., None] * o_win
    return o.transpose(0, 3, 1, 2, 4).reshape(B, Tq, W_B).astype(q_raw.dtype)


def contiguous_fetch(kv):
    bi = jnp.arange(kv.shape[0])[:, None, None, None]
    gi = jnp.arange(KV_B)[None, :, None, None]

    def fetch(kpos):
        rows = kv[bi, kpos, :, gi]
        return rows[..., 0, :], rows[..., 1, :]
    return fetch


def paged_fetch(pool, l, page_table, kv_new):
    B, Tn = kv_new.shape[:2]
    bi = jnp.arange(B)[:, None, None, None]
    gi = jnp.arange(KV_B)[None, :, None, None]
    li = jnp.full((1, 1, 1, 1), l, jnp.int32)

    def fetch(kpos):
        in_past = kpos < PAST_LEN
        pc = jnp.minimum(kpos, PAST_LEN - 1)
        page = page_table[bi, pc // PAGE_SIZE]
        past_rows = pool[li, page, pc % PAGE_SIZE, :, gi].astype(kv_new.dtype)
        new_rows = kv_new[bi, jnp.clip(kpos - PAST_LEN, 0, Tn - 1), :, gi]
        rows = jnp.where(in_past[..., None, None], past_rows, new_rows)
        return rows[..., 0, :], rows[..., 1, :]
    return fetch


MOE_ROWS = 512
MOE_UP_COLS = 256
MOE_DOWN_COLS = 512


def _moe_up_kernel(blk_ref, ocol_ref, exp_ref, wcol_ref, flag_ref, x_ref, wg_ref, wu_ref, bg_ref, bu_ref,
                   h_ref, wg_bf, wu_bf):
    flag = flag_ref[pl.program_id(0)]

    @pl.when(flag >= 2)
    def _():
        @pl.when(flag == 3)
        def _():
            wg_bf[...] = wg_ref[...].astype(jnp.bfloat16)
            wu_bf[...] = wu_ref[...].astype(jnp.bfloat16)
        x = x_ref[...]
        gate = jnp.minimum(jnp.dot(x, wg_bf[...], preferred_element_type=jnp.float32) + bg_ref[...], SWIGLU_LIMIT)
        up = jnp.clip(jnp.dot(x, wu_bf[...], preferred_element_type=jnp.float32) + bu_ref[...],
                      -SWIGLU_LIMIT, SWIGLU_LIMIT)
        h_ref[...] = ((up + 1.0) * gate * jax.nn.sigmoid(SWIGLU_ALPHA * gate)).astype(h_ref.dtype)

    @pl.when(flag < 2)
    def _():
        h_ref[...] = jnp.zeros(h_ref.shape, h_ref.dtype)


def _moe_down_kernel(blk_ref, ocol_ref, exp_ref, wcol_ref, flag_ref, h_ref, wd_ref, bd_ref, y_ref, wd_bf):
    flag = flag_ref[pl.program_id(0)]

    @pl.when(flag >= 2)
    def _():
        @pl.when(flag == 3)
        def _():
            wd_bf[...] = wd_ref[...].astype(jnp.bfloat16)
        y_ref[...] = jnp.dot(h_ref[...], wd_bf[...], preferred_element_type=jnp.float32) + bd_ref[...]

    @pl.when(flag < 2)
    def _():
        y_ref[...] = jnp.zeros(y_ref.shape, y_ref.dtype)


def _moe_expert_matmul(body, nblk_e, first_blk, n_blocks, n_col, tn, rows, weights, biases, l, out_dtype, name):
    tm = MOE_ROWS
    K = rows[0].shape[1]
    N = n_col * tn
    n_items = n_blocks * n_col
    i = jnp.arange(n_items, dtype=jnp.int32)
    total_blocks = jnp.sum(nblk_e)
    total = total_blocks * n_col
    valid = i < total
    iv = jnp.minimum(i, total - 1)
    e = jnp.minimum(jnp.searchsorted(jnp.cumsum(nblk_e) * n_col, iv, side='right'), N_EXPERTS - 1).astype(jnp.int32)
    j = iv - first_blk[e] * n_col
    nb = jnp.maximum(nblk_e[e], 1)
    wcol = j // nb
    mb = j % nb
    spare = i - total
    blk = jnp.where(valid, first_blk[e] + mb, total_blocks + spare // n_col).astype(jnp.int32)
    ocol = jnp.where(valid, wcol, spare % n_col).astype(jnp.int32)
    flag = jnp.where(valid, 2 + (mb == 0), 0).astype(jnp.int32)
    wcol = wcol.astype(jnp.int32)

    w_spec = pl.BlockSpec((None, None, K, tn), lambda s, blk, ocol, e, wcol, flag: (l, e[s], 0, wcol[s]))
    b_spec = pl.BlockSpec((None, None, 1, tn), lambda s, blk, ocol, e, wcol, flag: (l, e[s], 0, wcol[s]))
    grid_spec = pltpu.PrefetchScalarGridSpec(
        num_scalar_prefetch=5,
        grid=(n_items,),
        in_specs=([pl.BlockSpec((tm, K), lambda s, blk, ocol, e, wcol, flag: (blk[s], 0))]
                  + [w_spec] * len(weights) + [b_spec] * len(biases)),
        out_specs=pl.BlockSpec((tm, tn), lambda s, blk, ocol, e, wcol, flag: (blk[s], ocol[s])),
        scratch_shapes=[pltpu.VMEM((K, tn), jnp.bfloat16)] * len(weights),
    )
    return pl.pallas_call(
        body,
        out_shape=jax.ShapeDtypeStruct((n_blocks * tm, N), out_dtype),
        grid_spec=grid_spec,
        compiler_params=pltpu.CompilerParams(dimension_semantics=("arbitrary",),
                                             vmem_limit_bytes=V7X_VMEM_LIMIT_BYTES),
        name=name,
    )(blk, ocol, e, wcol, flag, *rows, *weights,
      *[b.reshape(b.shape[0], b.shape[1], 1, b.shape[2]) for b in biases])


def moe_ffn(x, l, router_w, router_b, w_gate, b_gate, w_up, b_up, w_down, b_down):
    shape = x.shape
    xt = x.reshape(-1, D_MODEL)
    T = xt.shape[0]
    logits = jnp.dot(xt, router_w[l]).astype(jnp.float32) + router_b[l].astype(jnp.float32)
    top_val, top_exp = lax.top_k(logits, TOP_K)
    weights = jax.nn.softmax(top_val, axis=-1)
    n = T * TOP_K
    blk = MOE_ROWS
    flat_e = top_exp.reshape(-1)
    order = jnp.argsort(flat_e)
    e_sorted = flat_e[order]
    counts = jnp.zeros((N_EXPERTS,), jnp.int32).at[flat_e].add(1)
    nblk_e = (counts + blk - 1) // blk
    start = jnp.cumsum(counts) - counts
    first_blk = jnp.cumsum(nblk_e) - nblk_e
    dest = first_blk[e_sorted] * blk + jnp.arange(n, dtype=jnp.int32) - start[e_sorted]
    n_blocks = -(-(n + N_EXPERTS * (blk - 1)) // blk)
    slot_tok = jnp.full((n_blocks * blk,), T, jnp.int32).at[dest].set((order // TOP_K).astype(jnp.int32))
    slot_of = jnp.zeros((n,), jnp.int32).at[order].set(dest).reshape(T, TOP_K)
    x_ext = jnp.concatenate([xt, jnp.zeros((1, D_MODEL), xt.dtype)], axis=0)
    x_rows = x_ext[slot_tok].astype(jnp.bfloat16)

    hb = _moe_expert_matmul(_moe_up_kernel, nblk_e, first_blk, n_blocks, D_FF // MOE_UP_COLS, MOE_UP_COLS,
                            [x_rows], [w_gate, w_up], [b_gate, b_up], l, jnp.bfloat16, "moe_gate_up")
    y_rows = _moe_expert_matmul(_moe_down_kernel, nblk_e, first_blk, n_blocks, D_MODEL // MOE_DOWN_COLS,
                                MOE_DOWN_COLS, [hb], [w_down], [b_down], l, jnp.float32, "moe_down")
    y = jnp.sum(y_rows[slot_of] * weights[..., None], axis=1)
    return y.astype(x.dtype).reshape(shape)


def token_mixers(h, l, qpos, prm, past):
    B, T, _ = h.shape
    f32 = jnp.float32
    proj = _dot2(h, prm['w_in'][l])
    (a_qkv, a_z, a_a, a_b, b_q, b_kv, b_g, c_q, c_k, c_v, c_if, c_o, merge_g) = jnp.split(proj, IN_OFFSETS, axis=-1)
    if past is None:
        conv_prev = jnp.zeros((B, CONV_W - 1, 3 * W_A), h.dtype)
        S0 = jnp.zeros((B, H_A, DK_A, DV_A), f32)
        C0 = jnp.zeros((B, H_C, DK_C, DV_C), f32)
        n0 = jnp.zeros((B, H_C, DK_C), f32)
        m0 = jnp.full((B, H_C), NEG, f32)
    else:
        conv_prev = past['gdn_conv'][l]
        S0 = past['gdn_S'][l]
        C0 = past['mlstm_C'][l]
        n0 = past['mlstm_n'][l]
        m0 = past['mlstm_m'][l]

    o_a, S_a, conv_a = gdn_mixer(a_qkv, a_z, a_a, a_b, conv_prev, S0, prm['gdn_conv_w'][l],
                                 prm['gdn_a_log'][l], prm['gdn_dt_bias'][l], prm['gdn_norm_w'][l])

    kv_new = b_kv.reshape(B, T, 3, 2, KV_B, DH_B)
    kv_cmp, kv_slc, kv_win = kv_new[:, :, 0], kv_new[:, :, 1], kv_new[:, :, 2]
    if past is None:
        kv_cmp_full = kv_cmp
        fetch = contiguous_fetch(kv_slc)
        nq = T // WIN_QBLK
        span = WIN_QBLK + WINDOW
        kidx = jnp.arange(nq)[:, None] * WIN_QBLK + jnp.arange(span)[None, :]
        win_blocks = jnp.pad(kv_win, ((0, 0), (WINDOW, 0), (0, 0), (0, 0), (0, 0)))[:, kidx]
        win_kpos = kidx - WINDOW
        win_all = kv_win
    else:
        past_cmp = past['kv_cmp'][l, past['page_table']]
        kv_cmp_full = jnp.concatenate([past_cmp.reshape(B, -1, 2, KV_B, DH_B).astype(kv_cmp.dtype), kv_cmp], axis=1)
        fetch = paged_fetch(past['kv_slc'], l, past['page_table'], kv_slc)
        buf = past['kv_win'][l].astype(kv_win.dtype)
        wb = buf.shape[1]
        win_all = jnp.concatenate([buf, kv_win], axis=1)
        win_blocks = win_all[:, None]
        win_kpos = (PAST_LEN - wb + jnp.arange(wb + T))[None, :]
    o_b = nsa_mixer(b_q, qpos, b_g, kv_cmp_full, fetch, win_blocks, win_kpos,
                    prm['nsa_cmp_pe'][l], prm['nsa_cmp_w'][l], prm['rel_bias_table'],
                    contiguous_kv=b_kv if past is None else None)
    new_win = win_all[:, -min(WINDOW, win_all.shape[1]):]

    o_c, C, n, m = mlstm_mixer(c_q, c_k, c_v, c_if, c_o, C0, n0, m0, prm['mlstm_b_if'][l], prm['mlstm_norm_w'][l])

    g = jax.nn.sigmoid(merge_g.astype(f32).reshape(B, T, 3, D_MODEL))
    merged = (g[:, :, 0] * _dot2(o_a, prm['w_up_a'][l]) + g[:, :, 1] * _dot2(o_b, prm['w_up_b'][l])
              + g[:, :, 2] * _dot2(o_c, prm['w_up_c'][l]))
    out = _dot2(merged.astype(h.dtype), prm['w_out'][l])
    return out, (kv_cmp, kv_slc, new_win, S_a, conv_a, C, n, m)


def kernel(x_prompt, x_sample, cache_kv_cmp, cache_kv_slc, cache_kv_win, state_gdn_S, state_gdn_conv,
           state_mlstm_C, state_mlstm_n, state_mlstm_m, page_table,
           norm_mix, norm_ffn, norm_final, w_in, gdn_conv_w, gdn_a_log, gdn_dt_bias, gdn_norm_w,
           nsa_cmp_pe, nsa_cmp_w, rel_bias_table, mlstm_b_if, mlstm_norm_w,
           w_up_a, w_up_b, w_up_c, w_out, router_w, router_b,
           exp_w_gate, exp_b_gate, exp_w_up, exp_b_up, exp_w_down, exp_b_down):
    prm = {'w_in': w_in, 'gdn_conv_w': gdn_conv_w, 'gdn_a_log': gdn_a_log, 'gdn_dt_bias': gdn_dt_bias,
           'gdn_norm_w': gdn_norm_w, 'nsa_cmp_pe': nsa_cmp_pe, 'nsa_cmp_w': nsa_cmp_w,
           'rel_bias_table': rel_bias_table, 'mlstm_b_if': mlstm_b_if, 'mlstm_norm_w': mlstm_norm_w,
           'w_up_a': w_up_a, 'w_up_b': w_up_b, 'w_up_c': w_up_c, 'w_out': w_out}
    past = {'kv_cmp': cache_kv_cmp, 'kv_slc': cache_kv_slc, 'kv_win': cache_kv_win, 'gdn_S': state_gdn_S,
            'gdn_conv': state_gdn_conv, 'mlstm_C': state_mlstm_C, 'mlstm_n': state_mlstm_n,
            'mlstm_m': state_mlstm_m, 'page_table': page_table}

    qpos_p = jnp.arange(SEQ, dtype=jnp.int32)
    qpos_s = PAST_LEN + jnp.arange(DEC_SEQ, dtype=jnp.int32)
    n_p = BATCH * SEQ
    xp, xs = x_prompt, x_sample
    states_p, states_s = [], []
    for l in range(DEPTH):
        mix_p, st_p = token_mixers(rms_norm(xp, norm_mix[l]), l, qpos_p, prm, None)
        mix_s, st_s = token_mixers(rms_norm(xs, norm_mix[l]), l, qpos_s, prm, past)
        xp = xp + mix_p
        xs = xs + mix_s
        h_all = jnp.concatenate([rms_norm(xp, norm_ffn[l]).reshape(n_p, D_MODEL),
                                 rms_norm(xs, norm_ffn[l]).reshape(-1, D_MODEL)], axis=0)
        y_all = moe_ffn(h_all, l, router_w, router_b, exp_w_gate, exp_b_gate,
                        exp_w_up, exp_b_up, exp_w_down, exp_b_down)
        xp = xp + y_all[:n_p].reshape(xp.shape)
        xs = xs + y_all[n_p:].reshape(xs.shape)
        states_p.append(st_p)
        states_s.append(st_s)
    y_prompt = rms_norm(xp, norm_final)
    y_sample = rms_norm(xs, norm_final)
    kvc_p, kvs_p, win_p, gS_p, gconv_p, mC_p, mn_p, mm_p = [jnp.stack(s, axis=0) for s in zip(*states_p)]
    kvc_s, kvs_s, win_s, gS_s, gconv_s, mC_s, mn_s, mm_s = [jnp.stack(s, axis=0) for s in zip(*states_s)]
    return (y_prompt, y_sample, kvc_p, kvc_s, kvs_p, kvs_s, win_p, win_s, gS_p, gS_s,
            gconv_p, gconv_s, mC_p, mC_s, mn_p, mn_s, mm_p, mm_s)
```

```python
import functools
import math

import jax
import jax.numpy as jnp
import numpy as np
from jax import lax
from jax.experimental import pallas as pl
from jax.experimental.pallas import tpu as pltpu

D_MODEL = 4096
BATCH = 4
SEQ = 2048
DEPTH = 2
DEC_BATCH = 8
DEC_SEQ = 4
PAST_LEN = 16384
PAGE_SIZE = 128

H_A = D_MODEL // 512
DK_A = 128
DV_A = 128
W_A = H_A * DV_A
CONV_W = 4
GDN_CHUNK = 64
H_B = D_MODEL // 256
KV_B = 4
GRP_B = H_B // KV_B
DH_B = 128
W_B = H_B * DH_B
CMP_BLK = 32
SLC_BLK = 64
N_SEL = 16
WINDOW = 512
WIN_QBLK = 128
SLC_QBLK = 16
SEL_FORCE = 1e9
H_C = D_MODEL // 512
DK_C = 64
DV_C = 128
W_C = H_C * DV_C
MLSTM_CHUNK = 64
GATE_CAP = 15.0
REL_BUCKETS = 32
REL_EXACT = REL_BUCKETS // 2
REL_MAX_DIST = 128
N_EXPERTS = 32
TOP_K = 4
D_FF = D_MODEL // 2
SWIGLU_ALPHA = 1.702
SWIGLU_LIMIT = 7.0
MOE_BLOCK = 128
MOE_BLOCK_SMALL = 8

RMS_EPS = 1e-6
NEG = -1e30

IN_SIZES = (3 * W_A, W_A, H_A, H_A,
            W_B, 6 * KV_B * DH_B, 3 * H_B,
            H_C * DK_C, H_C * DK_C, W_C, 2 * H_C, W_C,
            3 * D_MODEL)
IN_OFFSETS = tuple(int(c) for c in np.cumsum(IN_SIZES)[:-1])
IN_COLS = int(sum(IN_SIZES))

V7X_VMEM_LIMIT_BYTES = 56 * 1024 * 1024


DENSE_ROWS = 1024
DENSE_COLS = 512


def _mm_kernel(x_ref, w_ref, o_ref, w_bf):
    @pl.when(pl.program_id(1) == 0)
    def _():
        w_bf[...] = w_ref[...].astype(jnp.bfloat16)

    o_ref[...] = jnp.dot(x_ref[...], w_bf[...], preferred_element_type=jnp.float32)


def _matmul(x, w):
    M, K = x.shape
    N = w.shape[1]
    tm = min(DENSE_ROWS, M)
    tn = min(DENSE_COLS, N)
    assert M % tm == 0
    return pl.pallas_call(
        _mm_kernel,
        out_shape=jax.ShapeDtypeStruct((M, N), jnp.float32),
        grid=(pl.cdiv(N, tn), M // tm),
        in_specs=[pl.BlockSpec((tm, K), lambda j, i: (i, 0)),
                  pl.BlockSpec((K, tn), lambda j, i: (0, j))],
        out_specs=pl.BlockSpec((tm, tn), lambda j, i: (i, j)),
        scratch_shapes=[pltpu.VMEM((K, tn), jnp.bfloat16)],
        compiler_params=pltpu.CompilerParams(
            dimension_semantics=("parallel", "arbitrary"),
            vmem_limit_bytes=V7X_VMEM_LIMIT_BYTES),
        name="dense_matmul",
    )(x, w)


def _dot2(x, w):
    lead = x.shape[:-1]
    return _matmul(x.reshape(-1, x.shape[-1]).astype(jnp.bfloat16), w).reshape(lead + (w.shape[1],))


def rms_norm(x, g):
    xf = x.astype(jnp.float32)
    y = xf * lax.rsqrt(jnp.mean(xf * xf, axis=-1, keepdims=True) + RMS_EPS)
    return (y * g.astype(jnp.float32)).astype(x.dtype)


def head_rms(x, w):
    return x * lax.rsqrt(jnp.mean(x * x, axis=-1, keepdims=True) + RMS_EPS) * w.astype(jnp.float32)


def l2_normalize(x):
    return x * lax.rsqrt(jnp.sum(x * x, axis=-1, keepdims=True) + RMS_EPS)


def softcap(x, cap):
    return cap * jnp.tanh(x / cap)


def masked_softmax(s, mask):
    p = jax.nn.softmax(jnp.where(mask, s, NEG), axis=-1)
    return jnp.where(mask, p, 0.0)


def rel_bucket(dist):
    n = jnp.maximum(dist, 0)
    nf = jnp.maximum(n, 1).astype(jnp.float32)
    ratio = jnp.log(nf / REL_EXACT) / math.log(REL_MAX_DIST / REL_EXACT)
    large = REL_EXACT + (ratio * (REL_BUCKETS - REL_EXACT)).astype(jnp.int32)
    return jnp.where(n < REL_EXACT, n, jnp.minimum(large, REL_BUCKETS - 1))


def rel_bias_heads(dist, table):
    b = table[rel_bucket(dist)].astype(jnp.float32).reshape(dist.shape + (KV_B, GRP_B))
    nd = dist.ndim
    return jnp.transpose(b, (nd, nd + 1) + tuple(range(nd)))


def pad_time(x, n_pad, value=0.0):
    if n_pad == 0:
        return x
    widths = [(0, 0)] * x.ndim
    widths[1] = (0, n_pad)
    return jnp.pad(x, widths, constant_values=value)


def to_chunks(x, chunk):
    B, T = x.shape[:2]
    x = x.reshape((B, T // chunk, chunk) + x.shape[2:])
    return jnp.moveaxis(x, (1, 3), (0, 2))


def from_chunks(x):
    x = jnp.moveaxis(x, (0, 2), (1, 3))
    return x.reshape((x.shape[0], x.shape[1] * x.shape[2]) + x.shape[3:])


GDN_HEADS_PER_STEP = 2
GDN_CHUNKS_PER_STEP = 8


def _gdn_prep_kernel(q_ref, k_ref, v_ref, gcol_ref, grow_ref, bcol_ref, u_ref, w_ref, attn_ref, *, n_group):
    L = GDN_CHUNK
    f32 = jnp.float32
    bf16 = jnp.bfloat16
    row = lax.broadcasted_iota(jnp.int32, (L, L), 0)
    col = lax.broadcasted_iota(jnp.int32, (L, L), 1)
    incl = row >= col
    strict = row > col
    eye = (row == col).astype(f32)

    def mm(a, b):
        return jnp.dot(a.astype(bf16), b.astype(bf16), preferred_element_type=f32)

    def mm_nt(a, b):
        return lax.dot_general(a.astype(bf16), b.astype(bf16), (((1,), (1,)), ((), ())), preferred_element_type=f32)

    def mm_exact(a, b):
        return jnp.dot(a, b, preferred_element_type=f32, precision=lax.Precision.HIGHEST)

    chunks = range(n_group)
    rows = [slice(c * L, (c + 1) * L) for c in chunks]
    k = [k_ref[0, rows[c], :] for c in chunks]
    gc = [gcol_ref[0, 0, rows[c], :] for c in chunks]
    bc = [bcol_ref[0, 0, rows[c], :] for c in chunks]
    decay = [jnp.exp(jnp.where(incl, gc[c] - grow_ref[0, 0, c:c + 1, :], NEG)) for c in chunks]
    kb = [k[c] * bc[c] for c in chunks]
    lower = [jnp.where(strict, mm_nt(kb[c], k[c]) * decay[c], 0.0) for c in chunks]
    tinv = [eye - lower[c] for c in chunks]
    power = [mm_exact(lower[c], lower[c]) for c in chunks]
    n_levels = L.bit_length() - 2
    for level in range(n_levels):
        tinv = [tinv[c] + mm_exact(tinv[c], power[c]) for c in chunks]
        if level < n_levels - 1:
            power = [mm_exact(power[c], power[c]) for c in chunks]
    for c in chunks:
        u_ref[0, rows[c], :] = mm(tinv[c], v_ref[0, rows[c], :] * bc[c])
        w_ref[0, rows[c], :] = mm(tinv[c], kb[c] * jnp.exp(gc[c]))
        attn_ref[0, 0, rows[c], :] = jnp.where(incl, mm_nt(q_ref[0, rows[c], :], k[c]) * decay[c], 0.0)


def _gdn_state_kernel(q_ref, k_ref, u_ref, w_ref, attn_ref, gcol_ref, grow_ref, s0_ref, o_ref, s_ref, *, n_chunks):
    L = GDN_CHUNK
    hp = GDN_HEADS_PER_STEP
    f32 = jnp.float32
    bf16 = jnp.bfloat16

    def mm(a, b):
        return jnp.dot(a.astype(bf16), b.astype(bf16), preferred_element_type=f32)

    s_ref[...] = s0_ref[...]

    def chunk(c, carry):
        t0 = pl.multiple_of(c * L, L)
        for h in range(hp):
            lanes = slice(h * DK_A, (h + 1) * DK_A)
            q = q_ref[0, pl.ds(t0, L), lanes]
            k = k_ref[0, pl.ds(t0, L), lanes]
            gc = gcol_ref[0, h, pl.ds(t0, L), :]
            g_last = grow_ref[0, h, pl.ds(c, 1), :][:, L - 1:L]
            S = s_ref[0, h]
            v_new = u_ref[0, pl.ds(t0, L), lanes] - mm(w_ref[0, pl.ds(t0, L), lanes], S)
            o_ref[0, pl.ds(t0, L), lanes] = mm(q * jnp.exp(gc), S) + mm(attn_ref[0, h, pl.ds(t0, L), :], v_new)
            k_dec = (k * jnp.exp(g_last - gc)).astype(bf16)
            s_ref[0, h] = S * jnp.exp(g_last) + lax.dot_general(
                k_dec, v_new.astype(bf16), (((0,), (0,)), ((), ())), preferred_element_type=f32)
        return carry

    lax.fori_loop(0, n_chunks, chunk, 0)


def gdn_recurrence(q, k, v, g, beta, S0):
    B, T, H, _ = q.shape
    L = GDN_CHUNK
    hp = GDN_HEADS_PER_STEP
    assert DK_A == DV_A and H % hp == 0 and L & (L - 1) == 0
    n_pad = (-T) % L
    q, k, v, g, beta = [pad_time(t, n_pad) for t in (q, k, v, g, beta)]
    Tp = T + n_pad
    nc = Tp // L
    grp = math.gcd(nc, GDN_CHUNKS_PER_STEP)
    gcum = jnp.cumsum(g.reshape(B, nc, L, H), axis=2).transpose(0, 3, 1, 2)
    gcol = gcum.reshape(B, H, Tp, 1)
    bcol = beta.transpose(0, 2, 1).reshape(B, H, Tp, 1)
    q2, k2, v2 = q.reshape(B, Tp, H * DK_A), k.reshape(B, Tp, H * DK_A), v.reshape(B, Tp, H * DV_A)

    rows_spec = pl.BlockSpec((1, grp * L, DK_A), lambda b, h, c: (b, c, h))
    rcol_spec = pl.BlockSpec((1, 1, grp * L, 1), lambda b, h, c: (b, h, c, 0))
    u, w, attn = pl.pallas_call(
        functools.partial(_gdn_prep_kernel, n_group=grp),
        out_shape=(jax.ShapeDtypeStruct((B, Tp, H * DV_A), jnp.float32),
                   jax.ShapeDtypeStruct((B, Tp, H * DK_A), jnp.float32),
                   jax.ShapeDtypeStruct((B, H, Tp, L), jnp.float32)),
        grid=(B, H, nc // grp),
        in_specs=[rows_spec, rows_spec, rows_spec, rcol_spec,
                  pl.BlockSpec((1, 1, grp, L), lambda b, h, c: (b, h, c, 0)), rcol_spec],
        out_specs=(rows_spec, rows_spec, pl.BlockSpec((1, 1, grp * L, L), lambda b, h, c: (b, h, c, 0))),
        compiler_params=pltpu.CompilerParams(dimension_semantics=("parallel", "parallel", "parallel"),
                                             vmem_limit_bytes=V7X_VMEM_LIMIT_BYTES),
        name="gdn_chunk_prep",
    )(q2, k2, v2, gcol, gcum, bcol)

    seq_spec = pl.BlockSpec((1, Tp, hp * DK_A), lambda b, h: (b, 0, h))
    state_spec = pl.BlockSpec((1, hp, DK_A, DV_A), lambda b, h: (b, h, 0, 0))
    o, S = pl.pallas_call(
        functools.partial(_gdn_state_kernel, n_chunks=nc),
        out_shape=(jax.ShapeDtypeStruct((B, Tp, H * DV_A), jnp.float32),
                   jax.ShapeDtypeStruct((B, H, DK_A, DV_A), jnp.float32)),
        grid=(B, H // hp),
        in_specs=[seq_spec, seq_spec, seq_spec, seq_spec,
                  pl.BlockSpec((1, hp, Tp, L), lambda b, h: (b, h, 0, 0)),
                  pl.BlockSpec((1, hp, Tp, 1), lambda b, h: (b, h, 0, 0)),
                  pl.BlockSpec((1, hp, nc, L), lambda b, h: (b, h, 0, 0)), state_spec],
        out_specs=(seq_spec, state_spec),
        compiler_params=pltpu.CompilerParams(dimension_semantics=("parallel", "parallel"),
                                             vmem_limit_bytes=V7X_VMEM_LIMIT_BYTES),
        name="gdn_state",
    )(q2, k2, u, w, attn, gcol, gcum, S0)
    return o.reshape(B, Tp, H, DV_A)[:, :T], S


def gdn_mixer(qkv_raw, z, a, b, conv_prev, S0, conv_w, a_log, dt_bias, norm_w):
    B, T, _ = qkv_raw.shape
    xp = jnp.concatenate([conv_prev.astype(qkv_raw.dtype), qkv_raw], axis=1)
    conv = sum(xp[:, j:j + T] * conv_w[j] for j in range(CONV_W))
    qkv = jax.nn.silu(conv.astype(jnp.float32)).reshape(B, T, 3, H_A, DK_A)
    q = l2_normalize(qkv[:, :, 0]) * DK_A ** -0.5
    k = l2_normalize(qkv[:, :, 1])
    v = qkv[:, :, 2]
    beta = jax.nn.sigmoid(b.astype(jnp.float32))
    g = -jnp.exp(a_log.astype(jnp.float32)) * jax.nn.softplus(a.astype(jnp.float32) + dt_bias.astype(jnp.float32))
    o, S = gdn_recurrence(q, k, v, g, beta, S0.astype(jnp.float32))
    o = head_rms(o, norm_w) * jax.nn.silu(z.astype(jnp.float32).reshape(B, T, H_A, DV_A))
    return o.reshape(B, T, W_A).astype(qkv_raw.dtype), S, xp[:, T:]


def mlstm_recurrence(q, k, v, ig, lf, C0, n0, m0):
    T = q.shape[1]
    L = MLSTM_CHUNK
    n_pad = (-T) % L
    q, k, v, lf = [to_chunks(pad_time(t, n_pad), L) for t in (q, k, v, lf)]
    ig = to_chunks(pad_time(ig, n_pad, NEG), L)
    incl = jnp.tril(jnp.ones((L, L), bool))

    def step(carry, inp):
        C, n, m = carry
        qi, ki, vi, ii, fi = inp
        b = jnp.cumsum(fi, axis=-1)
        dmat = jnp.where(incl, b[..., :, None] - b[..., None, :] + ii[..., None, :], NEG)
        inter = b + m[..., None]
        m_tok = jnp.maximum(jnp.max(dmat, axis=-1), inter)
        s = jnp.einsum('bhid,bhjd->bhij', qi, ki) * jnp.exp(dmat - m_tok[..., None])
        w_inter = jnp.exp(inter - m_tok)
        num = w_inter[..., None] * jnp.einsum('bhid,bhde->bhie', qi, C) + jnp.einsum('bhij,bhje->bhie', s, vi)
        den = w_inter * jnp.einsum('bhid,bhd->bhi', qi, n) + jnp.sum(s, axis=-1)
        h = num / jnp.maximum(jnp.abs(den), jnp.exp(-m_tok))[..., None]
        b_last = b[..., -1]
        wk_log = b_last[..., None] - b + ii
        m_new = jnp.maximum(b_last + m, jnp.max(wk_log, axis=-1))
        wk = jnp.exp(wk_log - m_new[..., None])
        dec = jnp.exp(b_last + m - m_new)
        C = dec[..., None, None] * C + jnp.einsum('bhj,bhjd,bhje->bhde', wk, ki, vi)
        n = dec[..., None] * n + jnp.einsum('bhj,bhjd->bhd', wk, ki)
        return (C, n, m_new), h

    (C, n, m), h = lax.scan(step, (C0, n0, m0), (q, k, v, ig, lf))
    return from_chunks(h)[:, :T], C, n, m


def mlstm_mixer(q_raw, k_raw, v_raw, if_raw, o_raw, C0, n0, m0, b_if, norm_w):
    B, T, _ = q_raw.shape
    f32 = jnp.float32
    q = q_raw.astype(f32).reshape(B, T, H_C, DK_C) * DK_C ** -0.5
    k = k_raw.astype(f32).reshape(B, T, H_C, DK_C)
    v = v_raw.astype(f32).reshape(B, T, H_C, DV_C)
    pre = softcap(if_raw.astype(f32).reshape(B, T, 2, H_C) + b_if.astype(f32), GATE_CAP)
    ig = pre[:, :, 0]
    lf = jax.nn.log_sigmoid(pre[:, :, 1])
    h, C, n, m = mlstm_recurrence(q, k, v, ig, lf, C0.astype(f32), n0.astype(f32), m0.astype(f32))
    h = head_rms(h, norm_w).reshape(B, T, W_C) * jax.nn.sigmoid(o_raw.astype(f32))
    return h.astype(q_raw.dtype), C, n, m


def cmp_pool(rows, pe):
    lead = rows.shape[:-4]
    n = rows.shape[-4] // CMP_BLK
    blocks = rows.reshape(lead + (n, CMP_BLK, 2, KV_B, DH_B))
    return jnp.mean(blocks * jnp.transpose(pe, (1, 0, 2))[:, :, None, :], axis=-4)


def nsa_compressed(q, qpos, pooled, w, table):
    nb = pooled.shape[1]
    summ = jnp.einsum('bncgd,cde->bncge', pooled, w)
    end_pos = (jnp.arange(nb) + 1) * CMP_BLK - 1
    dist = qpos[:, None] - end_pos[None, :]
    s = (jnp.einsum('bgrqd,bngd->bgrqn', q, summ[:, :, 0]).astype(jnp.float32) * DH_B ** -0.5
         + rel_bias_heads(dist, table))
    p = masked_softmax(s, dist >= 0)
    o = jnp.einsum('bgrqn,bngd->bgrqd', p.astype(summ.dtype), summ[:, :, 1])
    return o, p


def nsa_select(p_cmp, qpos, total_len):
    n_slc = -(-total_len // SLC_BLK)
    ratio = SLC_BLK // CMP_BLK
    imp = jnp.sum(p_cmp, axis=2)
    imp = jnp.pad(imp, ((0, 0), (0, 0), (0, 0), (0, n_slc * ratio - imp.shape[-1])))
    imp = imp.reshape(imp.shape[:-1] + (n_slc, ratio)).sum(-1)
    blk = jnp.arange(n_slc)[None, :]
    cur = (qpos // SLC_BLK)[:, None]
    valid = blk <= cur
    forced = (blk == 0) | (blk == cur) | (blk == cur - 1)
    score = jnp.where(valid & forced, SEL_FORCE, jnp.where(valid, imp, -SEL_FORCE))
    _, idx = lax.top_k(score, min(N_SEL, n_slc))
    return idx


def nsa_selected(q, qpos, idx, fetch, table):
    B, _, _, Tq, _ = q.shape
    n_sel = idx.shape[-1]
    qb = math.gcd(Tq, SLC_QBLK)
    nq = Tq // qb
    q_ch = jnp.moveaxis(q.reshape(B, KV_B, GRP_B, nq, qb, DH_B), 3, 0)
    pos_ch = qpos.reshape(nq, qb)
    idx_ch = jnp.moveaxis(idx.reshape(B, KV_B, nq, qb, n_sel), 2, 0)
    table_g = table.reshape(REL_BUCKETS, KV_B, GRP_B)
    gi = jnp.arange(KV_B)[None, :, None, None]
    offs = jnp.arange(SLC_BLK)

    def one_block(args):
        qc, pc, ic = args
        kpos = (ic[..., None] * SLC_BLK + offs).reshape(B, KV_B, qb, n_sel * SLC_BLK)
        kc, vc = fetch(kpos)
        dist = pc[None, None, :, None] - kpos
        bias = jnp.moveaxis(table_g[rel_bucket(dist), gi].astype(jnp.float32), -1, 2)
        s = jnp.einsum('bgrqd,bgqkd->bgrqk', qc, kc).astype(jnp.float32) * DH_B ** -0.5 + bias
        p = masked_softmax(s, (dist >= 0)[:, :, None])
        return jnp.einsum('bgrqk,bgqkd->bgrqd', p.astype(vc.dtype), vc)

    o = lax.map(one_block, (q_ch, pos_ch, idx_ch))
    return jnp.moveaxis(o, 0, 3).reshape(B, KV_B, GRP_B, Tq, DH_B)


NSA_TILE = 256
NSA_SUB_ROWS = 128
SLC_SHIFT = SLC_BLK.bit_length() - 1
assert 1 << SLC_SHIFT == SLC_BLK


def _nsa_flash_kernel(cfar_ref, q_ref, k_ref, v_ref, *rest, n_slc, window):
    if window:
        band_ref, o_ref, m_ref, l_ref, acc_ref = rest
    else:
        sel_ref, band_ref, o_ref, m_ref, l_ref, acc_ref = rest
        selt = sel_ref[0, 0].astype(jnp.bfloat16)
    g = pl.program_id(1)
    qi = pl.program_id(2)
    tq = tk = NSA_TILE
    bf16 = jnp.bfloat16
    f32 = jnp.float32
    qs = [q_ref[0, :, r * DH_B:(r + 1) * DH_B].astype(bf16) for r in range(GRP_B)]

    m_ref[...] = jnp.full(m_ref.shape, NEG, f32)
    l_ref[...] = jnp.zeros(l_ref.shape, f32)
    acc_ref[...] = jnp.zeros(acc_ref.shape, f32)

    def process(j, bias_of_head, triangle):
        k0 = pl.multiple_of(j * tk, tk)
        kt = k_ref[0, pl.ds(k0, tk), :].astype(bf16)
        vt = v_ref[0, pl.ds(k0, tk), :].astype(bf16)
        mask = None
        if triangle is not None:
            row = lax.broadcasted_iota(jnp.int32, (tq, tk), 0)
            col = lax.broadcasted_iota(jnp.int32, (tq, tk), 1)
            mask = row >= col if triangle == "causal" else col > row
        if not window:
            key_blk = jnp.right_shift(k0 + lax.broadcasted_iota(jnp.int32, (n_slc, tk), 1), SLC_SHIFT)
            expand = (key_blk == lax.broadcasted_iota(jnp.int32, (n_slc, tk), 0)).astype(bf16)
            chosen = jnp.dot(selt, expand, preferred_element_type=f32) > 0.5
            mask = chosen if mask is None else jnp.logical_and(mask, chosen)
        for r in range(GRP_B):
            for r0 in range(0, tq, NSA_SUB_ROWS):
                rows = slice(r0, r0 + NSA_SUB_ROWS)
                s = lax.dot_general(qs[r][rows], kt, (((1,), (1,)), ((), ())), preferred_element_type=f32)
                s = s * DH_B ** -0.5 + bias_of_head(r, rows)
                if mask is not None:
                    s = jnp.where(mask[rows], s, NEG)
                m_prev = m_ref[r, rows]
                m_next = jnp.maximum(m_prev, jnp.max(s, axis=-1, keepdims=True))
                p = jnp.exp(s - m_next)
                if mask is not None:
                    p = jnp.where(mask[rows], p, 0.0)
                alpha = jnp.exp(m_prev - m_next)
                l_ref[r, rows] = alpha * l_ref[r, rows] + jnp.sum(p, axis=-1, keepdims=True)
                acc_ref[r, rows] = alpha * acc_ref[r, rows] + jnp.dot(p.astype(bf16), vt,
                                                                      preferred_element_type=f32)
                m_ref[r, rows] = m_next

    def far_bias(r, rows):
        return cfar_ref[g * GRP_B + r]

    if window:
        @pl.when(qi >= 2)
        def _():
            process(qi - 2, far_bias, "before")
    else:
        def far_tile(j, carry):
            process(j, far_bias, None)
            return carry

        lax.fori_loop(0, jnp.maximum(qi - 1, 0), far_tile, 0)

    @pl.when(qi >= 1)
    def _():
        process(qi - 1, lambda r, rows: band_ref[1, 0, r, rows], None)

    process(qi, lambda r, rows: band_ref[0, 0, r, rows], "causal")

    for r in range(GRP_B):
        o_ref[0, :, r * DH_B:(r + 1) * DH_B] = acc_ref[r] / l_ref[r]


def nsa_flash_contiguous(b_q, b_kv, table, branch, idx=None):
    B, T, _ = b_q.shape
    tq = tk = NSA_TILE
    window = idx is None
    assert T % tq == 0 and tk >= REL_MAX_DIST and tk % SLC_BLK == 0 and WINDOW == 2 * tk
    n_slc = -(-T // SLC_BLK)
    rel = jnp.arange(tq)[:, None] - jnp.arange(tk)[None, :]
    dist = jnp.stack([rel, rel + tk])
    band = table[rel_bucket(dist)].astype(jnp.float32).reshape(2, tq, tk, KV_B, GRP_B).transpose(0, 3, 4, 1, 2)
    cfar = table[rel_bucket(jnp.int32(2 * tk))].astype(jnp.float32)
    k_col = (branch * 2 + 0) * KV_B
    v_col = (branch * 2 + 1) * KV_B
    grp_w = GRP_B * DH_B
    sel_args, sel_specs = [], []
    if not window:
        sel_args = [jnp.any(idx[..., None] == jnp.arange(n_slc, dtype=idx.dtype), axis=-2).astype(jnp.float32)]
        sel_specs = [pl.BlockSpec((1, 1, tq, n_slc), lambda b, g, i: (b, g, i, 0))]
    return pl.pallas_call(
        functools.partial(_nsa_flash_kernel, n_slc=n_slc, window=window),
        out_shape=jax.ShapeDtypeStruct((B, T, W_B), jnp.float32),
        grid=(B, KV_B, T // tq),
        in_specs=[pl.BlockSpec(memory_space=pltpu.SMEM),
                  pl.BlockSpec((1, tq, grp_w), lambda b, g, i: (b, i, g)),
                  pl.BlockSpec((1, T, DH_B), lambda b, g, i: (b, 0, k_col + g)),
                  pl.BlockSpec((1, T, DH_B), lambda b, g, i: (b, 0, v_col + g))]
                 + sel_specs
                 + [pl.BlockSpec((2, 1, GRP_B, tq, tk), lambda b, g, i: (0, g, 0, 0, 0))],
        out_specs=pl.BlockSpec((1, tq, grp_w), lambda b, g, i: (b, i, g)),
        scratch_shapes=[pltpu.VMEM((GRP_B, tq, 1), jnp.float32),
                        pltpu.VMEM((GRP_B, tq, 1), jnp.float32),
                        pltpu.VMEM((GRP_B, tq, DH_B), jnp.float32)],
        compiler_params=pltpu.CompilerParams(
            dimension_semantics=("parallel", "parallel", "arbitrary"),
            vmem_limit_bytes=V7X_VMEM_LIMIT_BYTES),
        name="nsa_window" if window else "nsa_selected",
    )(cfar, b_q, b_kv, b_kv, *sel_args, band)


def nsa_window(q, qpos, kb, kpos, table):
    dist = qpos[:, :, None] - kpos[:, None, :]
    mask = (dist >= 0) & (dist < WINDOW) & (kpos[:, None, :] >= 0)
    s = (jnp.einsum('bgrnqd,bnkgd->bgrnqk', q, kb[:, :, :, 0]).astype(jnp.float32) * DH_B ** -0.5
         + rel_bias_heads(dist, table))
    p = masked_softmax(s, mask)
    return jnp.einsum('bgrnqk,bnkgd->bgrnqd', p.astype(kb.dtype), kb[:, :, :, 1])


def nsa_mixer(q_raw, qpos, gate_raw, pooled_cmp, total_len, fetch, win_blocks, win_kpos, cmp_w, table,
              contiguous_kv=None):
    B, Tq, _ = q_raw.shape
    q = q_raw.reshape(B, Tq, KV_B, GRP_B, DH_B).transpose(0, 2, 3, 1, 4)
    o_cmp, p_cmp = nsa_compressed(q, qpos, pooled_cmp, cmp_w, table)
    idx = nsa_select(p_cmp, qpos, total_len)
    if contiguous_kv is None:
        o_slc = nsa_selected(q, qpos, idx, fetch, table)
        nq = win_blocks.shape[1]
        o_win = nsa_window(q.reshape(B, KV_B, GRP_B, nq, Tq // nq, DH_B), qpos.reshape(nq, Tq // nq),
                           win_blocks, win_kpos, table).reshape(B, KV_B, GRP_B, Tq, DH_B)
    else:
        o_slc = nsa_flash_contiguous(q_raw, contiguous_kv, table, 1, idx)
        o_slc = o_slc.reshape(B, Tq, KV_B, GRP_B, DH_B).transpose(0, 2, 3, 1, 4)
        o_win = nsa_flash_contiguous(q_raw, contiguous_kv, table, 2)
        o_win = o_win.reshape(B, Tq, KV_B, GRP_B, DH_B).transpose(0, 2, 3, 1, 4)
    g = jax.nn.sigmoid(gate_raw.astype(jnp.float32).reshape(B, Tq, 3, KV_B, GRP_B)).transpose(2, 0, 3, 4, 1)
    o = g[0][..., None] * o_cmp + g[1][..., None] * o_slc + g[2][..., None] * o_win
    return o.transpose(0, 3, 1, 2, 4).reshape(B, Tq, W_B).astype(q_raw.dtype)


def contiguous_fetch(kv):
    bi = jnp.arange(kv.shape[0])[:, None, None, None]
    gi = jnp.arange(KV_B)[None, :, None, None]

    def fetch(kpos):
        rows = kv[bi, kpos, :, gi]
        return rows[..., 0, :], rows[..., 1, :]
    return fetch


def paged_fetch(pool, l, page_table, kv_new):
    B, Tn = kv_new.shape[:2]
    bi = jnp.arange(B)[:, None, None, None]
    gi = jnp.arange(KV_B)[None, :, None, None]
    li = jnp.full((1, 1, 1, 1), l, jnp.int32)

    def fetch(kpos):
        in_past = kpos < PAST_LEN
        pc = jnp.minimum(kpos, PAST_LEN - 1)
        page = page_table[bi, pc // PAGE_SIZE]
        past_rows = pool[li, page, pc % PAGE_SIZE, :, gi].astype(kv_new.dtype)
        new_rows = kv_new[bi, jnp.clip(kpos - PAST_LEN, 0, Tn - 1), :, gi]
        rows = jnp.where(in_past[..., None, None], past_rows, new_rows)
        return rows[..., 0, :], rows[..., 1, :]
    return fetch


MOE_ROWS = 512
MOE_UP_COLS = 512
MOE_DOWN_COLS = 1024


def _moe_up_kernel(blk_ref, ocol_ref, exp_ref, wcol_ref, flag_ref, x_ref, wg_ref, wu_ref, bg_ref, bu_ref,
                   h_ref, wg_bf, wu_bf):
    flag = flag_ref[pl.program_id(0)]

    @pl.when(flag >= 2)
    def _():
        @pl.when(flag == 3)
        def _():
            wg_bf[...] = wg_ref[...].astype(jnp.bfloat16)
            wu_bf[...] = wu_ref[...].astype(jnp.bfloat16)
        x = x_ref[...]
        gate = jnp.minimum(jnp.dot(x, wg_bf[...], preferred_element_type=jnp.float32) + bg_ref[...], SWIGLU_LIMIT)
        up = jnp.clip(jnp.dot(x, wu_bf[...], preferred_element_type=jnp.float32) + bu_ref[...],
                      -SWIGLU_LIMIT, SWIGLU_LIMIT)
        h_ref[...] = ((up + 1.0) * gate * jax.nn.sigmoid(SWIGLU_ALPHA * gate)).astype(h_ref.dtype)

    @pl.when(flag < 2)
    def _():
        h_ref[...] = jnp.zeros(h_ref.shape, h_ref.dtype)


def _moe_down_kernel(blk_ref, ocol_ref, exp_ref, wcol_ref, flag_ref, h_ref, wd_ref, bd_ref, y_ref, wd_bf):
    flag = flag_ref[pl.program_id(0)]

    @pl.when(flag >= 2)
    def _():
        @pl.when(flag == 3)
        def _():
            wd_bf[...] = wd_ref[...].astype(jnp.bfloat16)
        y_ref[...] = jnp.dot(h_ref[...], wd_bf[...], preferred_element_type=jnp.float32) + bd_ref[...]

    @pl.when(flag < 2)
    def _():
        y_ref[...] = jnp.zeros(y_ref.shape, y_ref.dtype)


def _moe_expert_matmul(body, nblk_e, first_blk, n_blocks, n_col, tn, rows, weights, biases, l, out_dtype, name):
    tm = MOE_ROWS
    K = rows[0].shape[1]
    N = n_col * tn
    n_items = n_blocks * n_col
    i = jnp.arange(n_items, dtype=jnp.int32)
    total_blocks = jnp.sum(nblk_e)
    total = total_blocks * n_col
    valid = i < total
    iv = jnp.minimum(i, total - 1)
    e = jnp.minimum(jnp.searchsorted(jnp.cumsum(nblk_e) * n_col, iv, side='right'), N_EXPERTS - 1).astype(jnp.int32)
    j = iv - first_blk[e] * n_col
    nb = jnp.maximum(nblk_e[e], 1)
    wcol = j // nb
    mb = j % nb
    spare = i - total
    blk = jnp.where(valid, first_blk[e] + mb, total_blocks + spare // n_col).astype(jnp.int32)
    ocol = jnp.where(valid, wcol, spare % n_col).astype(jnp.int32)
    flag = jnp.where(valid, 2 + (mb == 0), 0).astype(jnp.int32)
    wcol = wcol.astype(jnp.int32)

    w_spec = pl.BlockSpec((None, None, K, tn), lambda s, blk, ocol, e, wcol, flag: (l, e[s], 0, wcol[s]))
    b_spec = pl.BlockSpec((None, None, 1, tn), lambda s, blk, ocol, e, wcol, flag: (l, e[s], 0, wcol[s]))
    grid_spec = pltpu.PrefetchScalarGridSpec(
        num_scalar_prefetch=5,
        grid=(n_items,),
        in_specs=([pl.BlockSpec((tm, K), lambda s, blk, ocol, e, wcol, flag: (blk[s], 0))]
                  + [w_spec] * len(weights) + [b_spec] * len(biases)),
        out_specs=pl.BlockSpec((tm, tn), lambda s, blk, ocol, e, wcol, flag: (blk[s], ocol[s])),
        scratch_shapes=[pltpu.VMEM((K, tn), jnp.bfloat16)] * len(weights),
    )
    return pl.pallas_call(
        body,
        out_shape=jax.ShapeDtypeStruct((n_blocks * tm, N), out_dtype),
        grid_spec=grid_spec,
        compiler_params=pltpu.CompilerParams(dimension_semantics=("arbitrary",),
                                             vmem_limit_bytes=V7X_VMEM_LIMIT_BYTES),
        name=name,
    )(blk, ocol, e, wcol, flag, *rows, *weights,
      *[b.reshape(b.shape[0], b.shape[1], 1, b.shape[2]) for b in biases])


def moe_ffn(x, l, router_w, router_b, w_gate, b_gate, w_up, b_up, w_down, b_down):
    shape = x.shape
    xt = x.reshape(-1, D_MODEL)
    T = xt.shape[0]
    logits = jnp.dot(xt, router_w[l]).astype(jnp.float32) + router_b[l].astype(jnp.float32)
    top_val, top_exp = lax.top_k(logits, TOP_K)
    weights = jax.nn.softmax(top_val, axis=-1)
    n = T * TOP_K
    blk = MOE_ROWS
    flat_e = top_exp.reshape(-1)
    order = jnp.argsort(flat_e)
    e_sorted = flat_e[order]
    counts = jnp.zeros((N_EXPERTS,), jnp.int32).at[flat_e].add(1)
    nblk_e = (counts + blk - 1) // blk
    start = jnp.cumsum(counts) - counts
    first_blk = jnp.cumsum(nblk_e) - nblk_e
    dest = first_blk[e_sorted] * blk + jnp.arange(n, dtype=jnp.int32) - start[e_sorted]
    n_blocks = -(-(n + N_EXPERTS * (blk - 1)) // blk)
    slot_tok = jnp.full((n_blocks * blk,), T, jnp.int32).at[dest].set((order // TOP_K).astype(jnp.int32))
    slot_of = jnp.zeros((n,), jnp.int32).at[order].set(dest).reshape(T, TOP_K)
    x_ext = jnp.concatenate([xt.astype(jnp.bfloat16), jnp.zeros((1, D_MODEL), jnp.bfloat16)], axis=0)
    x_rows = x_ext[slot_tok]

    hb = _moe_expert_matmul(_moe_up_kernel, nblk_e, first_blk, n_blocks, D_FF // MOE_UP_COLS, MOE_UP_COLS,
                            [x_rows], [w_gate, w_up], [b_gate, b_up], l, jnp.bfloat16, "moe_gate_up")
    y_rows = _moe_expert_matmul(_moe_down_kernel, nblk_e, first_blk, n_blocks, D_MODEL // MOE_DOWN_COLS,
                                MOE_DOWN_COLS, [hb], [w_down], [b_down], l, jnp.float32, "moe_down")
    y = jnp.sum(y_rows[slot_of] * weights[..., None], axis=1)
    return y.astype(x.dtype).reshape(shape)


def token_mixers(h, l, qpos, prm, past):
    B, T, _ = h.shape
    f32 = jnp.float32
    proj = _dot2(h, prm['w_in'][l])
    (a_qkv, a_z, a_a, a_b, b_q, b_kv, b_g, c_q, c_k, c_v, c_if, c_o, merge_g) = jnp.split(proj, IN_OFFSETS, axis=-1)
    if past is None:
        conv_prev = jnp.zeros((B, CONV_W - 1, 3 * W_A), h.dtype)
        S0 = jnp.zeros((B, H_A, DK_A, DV_A), f32)
        C0 = jnp.zeros((B, H_C, DK_C, DV_C), f32)
        n0 = jnp.zeros((B, H_C, DK_C), f32)
        m0 = jnp.full((B, H_C), NEG, f32)
    else:
        conv_prev = past['gdn_conv'][l]
        S0 = past['gdn_S'][l]
        C0 = past['mlstm_C'][l]
        n0 = past['mlstm_n'][l]
        m0 = past['mlstm_m'][l]

    o_a, S_a, conv_a = gdn_mixer(a_qkv, a_z, a_a, a_b, conv_prev, S0, prm['gdn_conv_w'][l],
                                 prm['gdn_a_log'][l], prm['gdn_dt_bias'][l], prm['gdn_norm_w'][l])

    kv_new = b_kv.reshape(B, T, 3, 2, KV_B, DH_B)
    kv_cmp, kv_slc, kv_win = kv_new[:, :, 0], kv_new[:, :, 1], kv_new[:, :, 2]
    cmp_pe = prm['nsa_cmp_pe'][l]
    if past is None:
        total_len = T
        pooled_cmp = cmp_pool(kv_cmp[:, :T // CMP_BLK * CMP_BLK], cmp_pe)
        fetch = win_blocks = win_kpos = None
        win_all = kv_win
    else:
        assert PAGE_SIZE % CMP_BLK == 0 and PAST_LEN % PAGE_SIZE == 0
        total_len = PAST_LEN + T
        page_pooled = cmp_pool(past['kv_cmp'][l].astype(kv_cmp.dtype), cmp_pe)
        pooled_cmp = page_pooled[past['page_table']].reshape(B, PAST_LEN // CMP_BLK, 2, KV_B, DH_B)
        n_new_blocks = total_len // CMP_BLK - PAST_LEN // CMP_BLK
        if n_new_blocks:
            pooled_cmp = jnp.concatenate([pooled_cmp, cmp_pool(kv_cmp[:, :n_new_blocks * CMP_BLK], cmp_pe)], axis=1)
        fetch = paged_fetch(past['kv_slc'], l, past['page_table'], kv_slc)
        buf = past['kv_win'][l].astype(kv_win.dtype)
        wb = buf.shape[1]
        win_all = jnp.concatenate([buf, kv_win], axis=1)
        win_blocks = win_all[:, None]
        win_kpos = (PAST_LEN - wb + jnp.arange(wb + T))[None, :]
    o_b = nsa_mixer(b_q, qpos, b_g, pooled_cmp, total_len, fetch, win_blocks, win_kpos,
                    prm['nsa_cmp_w'][l], prm['rel_bias_table'],
                    contiguous_kv=b_kv if past is None else None)
    new_win = win_all[:, -min(WINDOW, win_all.shape[1]):]

    o_c, C, n, m = mlstm_mixer(c_q, c_k, c_v, c_if, c_o, C0, n0, m0, prm['mlstm_b_if'][l], prm['mlstm_norm_w'][l])

    g = jax.nn.sigmoid(merge_g.astype(f32).reshape(B, T, 3, D_MODEL))
    merged = (g[:, :, 0] * _dot2(o_a, prm['w_up_a'][l]) + g[:, :, 1] * _dot2(o_b, prm['w_up_b'][l])
              + g[:, :, 2] * _dot2(o_c, prm['w_up_c'][l]))
    out = _dot2(merged.astype(h.dtype), prm['w_out'][l])
    return out, (kv_cmp, kv_slc, new_win, S_a, conv_a, C, n, m)


def kernel(x_prompt, x_sample, cache_kv_cmp, cache_kv_slc, cache_kv_win, state_gdn_S, state_gdn_conv,
           state_mlstm_C, state_mlstm_n, state_mlstm_m, page_table,
           norm_mix, norm_ffn, norm_final, w_in, gdn_conv_w, gdn_a_log, gdn_dt_bias, gdn_norm_w,
           nsa_cmp_pe, nsa_cmp_w, rel_bias_table, mlstm_b_if, mlstm_norm_w,
           w_up_a, w_up_b, w_up_c, w_out, router_w, router_b,
           exp_w_gate, exp_b_gate, exp_w_up, exp_b_up, exp_w_down, exp_b_down):
    prm = {'w_in': w_in, 'gdn_conv_w': gdn_conv_w, 'gdn_a_log': gdn_a_log, 'gdn_dt_bias': gdn_dt_bias,
           'gdn_norm_w': gdn_norm_w, 'nsa_cmp_pe': nsa_cmp_pe, 'nsa_cmp_w': nsa_cmp_w,
           'rel_bias_table': rel_bias_table, 'mlstm_b_if': mlstm_b_if, 'mlstm_norm_w': mlstm_norm_w,
           'w_up_a': w_up_a, 'w_up_b': w_up_b, 'w_up_c': w_up_c, 'w_out': w_out}
    past = {'kv_cmp': cache_kv_cmp, 'kv_slc': cache_kv_slc, 'kv_win': cache_kv_win, 'gdn_S': state_gdn_S,
            'gdn_conv': state_gdn_conv, 'mlstm_C': state_mlstm_C, 'mlstm_n': state_mlstm_n,
            'mlstm_m': state_mlstm_m, 'page_table': page_table}

    qpos_p = jnp.arange(SEQ, dtype=jnp.int32)
    qpos_s = PAST_LEN + jnp.arange(DEC_SEQ, dtype=jnp.int32)
    n_p = BATCH * SEQ
    xp, xs = x_prompt, x_sample
    states_p, states_s = [], []
    for l in range(DEPTH):
        mix_p, st_p = token_mixers(rms_norm(xp, norm_mix[l]), l, qpos_p, prm, None)
        mix_s, st_s = token_mixers(rms_norm(xs, norm_mix[l]), l, qpos_s, prm, past)
        xp = xp + mix_p
        xs = xs + mix_s
        h_all = jnp.concatenate([rms_norm(xp, norm_ffn[l]).reshape(n_p, D_MODEL),
                                 rms_norm(xs, norm_ffn[l]).reshape(-1, D_MODEL)], axis=0)
        y_all = moe_ffn(h_all, l, router_w, router_b, exp_w_gate, exp_b_gate,
                        exp_w_up, exp_b_up, exp_w_down, exp_b_down)
        xp = xp + y_all[:n_p].reshape(xp.shape)
        xs = xs + y_all[n_p:].reshape(xs.shape)
        states_p.append(st_p)
        states_s.append(st_s)
    y_prompt = rms_norm(xp, norm_final)
    y_sample = rms_norm(xs, norm_final)
    kvc_p, kvs_p, win_p, gS_p, gconv_p, mC_p, mn_p, mm_p = [jnp.stack(s, axis=0) for s in zip(*states_p)]
    kvc_s, kvs_s, win_s, gS_s, gconv_s, mC_s, mn_s, mm_s = [jnp.stack(s, axis=0) for s in zip(*states_s)]
    return (y_prompt, y_sample, kvc_p, kvc_s, kvs_p, kvs_s, win_p, win_s, gS_p, gS_s,
            gconv_p, gconv_s, mC_p, mC_s, mn_p, mn_s, mm_p, mm_s)
```

```python
import functools
import math

import jax
import jax.numpy as jnp
import numpy as np
from jax import lax
from jax.experimental import pallas as pl
from jax.experimental.pallas import tpu as pltpu

D_MODEL = 4096
BATCH = 4
SEQ = 2048
DEPTH = 2
DEC_BATCH = 8
DEC_SEQ = 4
PAST_LEN = 16384
PAGE_SIZE = 128

H_A = D_MODEL // 512
DK_A = 128
DV_A = 128
W_A = H_A * DV_A
CONV_W = 4
GDN_CHUNK = 64
H_B = D_MODEL // 256
KV_B = 4
GRP_B = H_B // KV_B
DH_B = 128
W_B = H_B * DH_B
CMP_BLK = 32
SLC_BLK = 64
N_SEL = 16
WINDOW = 512
WIN_QBLK = 128
SLC_QBLK = 16
SEL_FORCE = 1e9
H_C = D_MODEL // 512
DK_C = 64
DV_C = 128
W_C = H_C * DV_C
MLSTM_CHUNK = 64
GATE_CAP = 15.0
REL_BUCKETS = 32
REL_EXACT = REL_BUCKETS // 2
REL_MAX_DIST = 128
N_EXPERTS = 32
TOP_K = 4
D_FF = D_MODEL // 2
SWIGLU_ALPHA = 1.702
SWIGLU_LIMIT = 7.0
MOE_BLOCK = 128
MOE_BLOCK_SMALL = 8

RMS_EPS = 1e-6
NEG = -1e30

IN_SIZES = (3 * W_A, W_A, H_A, H_A,
            W_B, 6 * KV_B * DH_B, 3 * H_B,
            H_C * DK_C, H_C * DK_C, W_C, 2 * H_C, W_C,
            3 * D_MODEL)
IN_OFFSETS = tuple(int(c) for c in np.cumsum(IN_SIZES)[:-1])
IN_COLS = int(sum(IN_SIZES))

V7X_VMEM_LIMIT_BYTES = 56 * 1024 * 1024


DENSE_ROWS = 1024
DENSE_COLS = 512


def _mm_kernel(x_ref, w_ref, o_ref, w_bf):
    @pl.when(pl.program_id(1) == 0)
    def _():
        w_bf[...] = w_ref[...].astype(jnp.bfloat16)

    o_ref[...] = jnp.dot(x_ref[...], w_bf[...], preferred_element_type=jnp.float32)


def _matmul(x, w):
    M, K = x.shape
    N = w.shape[1]
    tm = min(DENSE_ROWS, M)
    tn = min(DENSE_COLS, N)
    assert M % tm == 0
    return pl.pallas_call(
        _mm_kernel,
        out_shape=jax.ShapeDtypeStruct((M, N), jnp.float32),
        grid=(pl.cdiv(N, tn), M // tm),
        in_specs=[pl.BlockSpec((tm, K), lambda j, i: (i, 0)),
                  pl.BlockSpec((K, tn), lambda j, i: (0, j))],
        out_specs=pl.BlockSpec((tm, tn), lambda j, i: (i, j)),
        scratch_shapes=[pltpu.VMEM((K, tn), jnp.bfloat16)],
        compiler_params=pltpu.CompilerParams(
            dimension_semantics=("parallel", "arbitrary"),
            vmem_limit_bytes=V7X_VMEM_LIMIT_BYTES),
        name="dense_matmul",
    )(x, w)


MERGE_ROWS = 512
MERGE_COLS = 512


def _merge_up_kernel(xa_ref, xb_ref, xc_ref, ga_ref, gb_ref, gc_ref, wa_ref, wb_ref, wc_ref, o_ref,
                     wa_bf, wb_bf, wc_bf):
    @pl.when(pl.program_id(1) == 0)
    def _():
        wa_bf[...] = wa_ref[...].astype(jnp.bfloat16)
        wb_bf[...] = wb_ref[...].astype(jnp.bfloat16)
        wc_bf[...] = wc_ref[...].astype(jnp.bfloat16)

    f32 = jnp.float32
    merged = (jax.nn.sigmoid(ga_ref[...]) * jnp.dot(xa_ref[...], wa_bf[...], preferred_element_type=f32)
              + jax.nn.sigmoid(gb_ref[...]) * jnp.dot(xb_ref[...], wb_bf[...], preferred_element_type=f32)
              + jax.nn.sigmoid(gc_ref[...]) * jnp.dot(xc_ref[...], wc_bf[...], preferred_element_type=f32))
    o_ref[...] = merged.astype(o_ref.dtype)


def merge_up(o_a, o_b, o_c, merge_g, w_a, w_b, w_c):
    M = o_a.shape[0]
    N = w_a.shape[1]
    tm = min(MERGE_ROWS, M)
    tn = MERGE_COLS
    assert M % tm == 0 and N % tn == 0
    n_col = N // tn
    xs = [o.astype(jnp.bfloat16) for o in (o_a, o_b, o_c)]
    ws = (w_a, w_b, w_c)

    def gate_spec(branch):
        return pl.BlockSpec((tm, tn), lambda j, i: (i, branch * n_col + j))

    return pl.pallas_call(
        _merge_up_kernel,
        out_shape=jax.ShapeDtypeStruct((M, N), jnp.bfloat16),
        grid=(n_col, M // tm),
        in_specs=([pl.BlockSpec((tm, x.shape[1]), lambda j, i: (i, 0)) for x in xs]
                  + [gate_spec(0), gate_spec(1), gate_spec(2)]
                  + [pl.BlockSpec((w.shape[0], tn), lambda j, i: (0, j)) for w in ws]),
        out_specs=pl.BlockSpec((tm, tn), lambda j, i: (i, j)),
        scratch_shapes=[pltpu.VMEM((w.shape[0], tn), jnp.bfloat16) for w in ws],
        compiler_params=pltpu.CompilerParams(
            dimension_semantics=("parallel", "arbitrary"),
            vmem_limit_bytes=V7X_VMEM_LIMIT_BYTES),
        name="merge_up",
    )(*xs, merge_g, merge_g, merge_g, *ws)


def _dot2(x, w):
    lead = x.shape[:-1]
    return _matmul(x.reshape(-1, x.shape[-1]).astype(jnp.bfloat16), w).reshape(lead + (w.shape[1],))


def rms_norm(x, g):
    xf = x.astype(jnp.float32)
    y = xf * lax.rsqrt(jnp.mean(xf * xf, axis=-1, keepdims=True) + RMS_EPS)
    return (y * g.astype(jnp.float32)).astype(x.dtype)


def head_rms(x, w):
    return x * lax.rsqrt(jnp.mean(x * x, axis=-1, keepdims=True) + RMS_EPS) * w.astype(jnp.float32)


def l2_normalize(x):
    return x * lax.rsqrt(jnp.sum(x * x, axis=-1, keepdims=True) + RMS_EPS)


def softcap(x, cap):
    return cap * jnp.tanh(x / cap)


def masked_softmax(s, mask):
    p = jax.nn.softmax(jnp.where(mask, s, NEG), axis=-1)
    return jnp.where(mask, p, 0.0)


def rel_bucket(dist):
    n = jnp.maximum(dist, 0)
    nf = jnp.maximum(n, 1).astype(jnp.float32)
    ratio = jnp.log(nf / REL_EXACT) / math.log(REL_MAX_DIST / REL_EXACT)
    large = REL_EXACT + (ratio * (REL_BUCKETS - REL_EXACT)).astype(jnp.int32)
    return jnp.where(n < REL_EXACT, n, jnp.minimum(large, REL_BUCKETS - 1))


def rel_bias_heads(dist, table):
    b = table[rel_bucket(dist)].astype(jnp.float32).reshape(dist.shape + (KV_B, GRP_B))
    nd = dist.ndim
    return jnp.transpose(b, (nd, nd + 1) + tuple(range(nd)))


def pad_time(x, n_pad, value=0.0):
    if n_pad == 0:
        return x
    widths = [(0, 0)] * x.ndim
    widths[1] = (0, n_pad)
    return jnp.pad(x, widths, constant_values=value)


def to_chunks(x, chunk):
    B, T = x.shape[:2]
    x = x.reshape((B, T // chunk, chunk) + x.shape[2:])
    return jnp.moveaxis(x, (1, 3), (0, 2))


def from_chunks(x):
    x = jnp.moveaxis(x, (0, 2), (1, 3))
    return x.reshape((x.shape[0], x.shape[1] * x.shape[2]) + x.shape[3:])


GDN_HEADS_PER_STEP = 2
GDN_CHUNKS_PER_STEP = 8


def _gdn_prep_kernel(q_ref, k_ref, v_ref, gcol_ref, grow_ref, bcol_ref, u_ref, w_ref, attn_ref, *, n_group):
    L = GDN_CHUNK
    f32 = jnp.float32
    bf16 = jnp.bfloat16
    row = lax.broadcasted_iota(jnp.int32, (L, L), 0)
    col = lax.broadcasted_iota(jnp.int32, (L, L), 1)
    incl = row >= col
    strict = row > col
    eye = (row == col).astype(f32)

    def mm(a, b):
        return jnp.dot(a.astype(bf16), b.astype(bf16), preferred_element_type=f32)

    def mm_nt(a, b):
        return lax.dot_general(a.astype(bf16), b.astype(bf16), (((1,), (1,)), ((), ())), preferred_element_type=f32)

    def mm_exact(a, b):
        return jnp.dot(a, b, preferred_element_type=f32, precision=lax.Precision.HIGHEST)

    chunks = range(n_group)
    rows = [slice(c * L, (c + 1) * L) for c in chunks]
    k = [k_ref[0, rows[c], :] for c in chunks]
    gc = [gcol_ref[0, 0, rows[c], :] for c in chunks]
    bc = [bcol_ref[0, 0, rows[c], :] for c in chunks]
    decay = [jnp.exp(jnp.where(incl, gc[c] - grow_ref[0, 0, c:c + 1, :], NEG)) for c in chunks]
    kb = [k[c] * bc[c] for c in chunks]
    lower = [jnp.where(strict, mm_nt(kb[c], k[c]) * decay[c], 0.0) for c in chunks]
    tinv = [eye - lower[c] for c in chunks]
    power = [mm_exact(lower[c], lower[c]) for c in chunks]
    n_levels = L.bit_length() - 2
    for level in range(n_levels):
        tinv = [tinv[c] + mm_exact(tinv[c], power[c]) for c in chunks]
        if level < n_levels - 1:
            power = [mm_exact(power[c], power[c]) for c in chunks]
    for c in chunks:
        u_ref[0, rows[c], :] = mm(tinv[c], v_ref[0, rows[c], :] * bc[c])
        w_ref[0, rows[c], :] = mm(tinv[c], kb[c] * jnp.exp(gc[c]))
        attn_ref[0, 0, rows[c], :] = jnp.where(incl, mm_nt(q_ref[0, rows[c], :], k[c]) * decay[c], 0.0)


def _gdn_state_kernel(q_ref, k_ref, u_ref, w_ref, attn_ref, gcol_ref, grow_ref, s0_ref, o_ref, s_ref, *, n_chunks):
    L = GDN_CHUNK
    hp = GDN_HEADS_PER_STEP
    f32 = jnp.float32
    bf16 = jnp.bfloat16

    def mm(a, b):
        return jnp.dot(a.astype(bf16), b.astype(bf16), preferred_element_type=f32)

    s_ref[...] = s0_ref[...]

    def chunk(c, carry):
        t0 = pl.multiple_of(c * L, L)
        for h in range(hp):
            lanes = slice(h * DK_A, (h + 1) * DK_A)
            q = q_ref[0, pl.ds(t0, L), lanes]
            k = k_ref[0, pl.ds(t0, L), lanes]
            gc = gcol_ref[0, h, pl.ds(t0, L), :]
            g_last = grow_ref[0, h, pl.ds(c, 1), :][:, L - 1:L]
            S = s_ref[0, h]
            v_new = u_ref[0, pl.ds(t0, L), lanes] - mm(w_ref[0, pl.ds(t0, L), lanes], S)
            o_ref[0, pl.ds(t0, L), lanes] = mm(q * jnp.exp(gc), S) + mm(attn_ref[0, h, pl.ds(t0, L), :], v_new)
            k_dec = (k * jnp.exp(g_last - gc)).astype(bf16)
            s_ref[0, h] = S * jnp.exp(g_last) + lax.dot_general(
                k_dec, v_new.astype(bf16), (((0,), (0,)), ((), ())), preferred_element_type=f32)
        return carry

    lax.fori_loop(0, n_chunks, chunk, 0)


def gdn_recurrence(q, k, v, g, beta, S0):
    B, T, H, _ = q.shape
    L = GDN_CHUNK
    hp = GDN_HEADS_PER_STEP
    assert DK_A == DV_A and H % hp == 0 and L & (L - 1) == 0
    n_pad = (-T) % L
    q, k, v, g, beta = [pad_time(t, n_pad) for t in (q, k, v, g, beta)]
    Tp = T + n_pad
    nc = Tp // L
    grp = math.gcd(nc, GDN_CHUNKS_PER_STEP)
    gcum = jnp.cumsum(g.reshape(B, nc, L, H), axis=2).transpose(0, 3, 1, 2)
    gcol = gcum.reshape(B, H, Tp, 1)
    bcol = beta.transpose(0, 2, 1).reshape(B, H, Tp, 1)
    q2, k2, v2 = q.reshape(B, Tp, H * DK_A), k.reshape(B, Tp, H * DK_A), v.reshape(B, Tp, H * DV_A)

    rows_spec = pl.BlockSpec((1, grp * L, DK_A), lambda b, h, c: (b, c, h))
    rcol_spec = pl.BlockSpec((1, 1, grp * L, 1), lambda b, h, c: (b, h, c, 0))
    u, w, attn = pl.pallas_call(
        functools.partial(_gdn_prep_kernel, n_group=grp),
        out_shape=(jax.ShapeDtypeStruct((B, Tp, H * DV_A), jnp.float32),
                   jax.ShapeDtypeStruct((B, Tp, H * DK_A), jnp.float32),
                   jax.ShapeDtypeStruct((B, H, Tp, L), jnp.float32)),
        grid=(B, H, nc // grp),
        in_specs=[rows_spec, rows_spec, rows_spec, rcol_spec,
                  pl.BlockSpec((1, 1, grp, L), lambda b, h, c: (b, h, c, 0)), rcol_spec],
        out_specs=(rows_spec, rows_spec, pl.BlockSpec((1, 1, grp * L, L), lambda b, h, c: (b, h, c, 0))),
        compiler_params=pltpu.CompilerParams(dimension_semantics=("parallel", "parallel", "parallel"),
                                             vmem_limit_bytes=V7X_VMEM_LIMIT_BYTES),
        name="gdn_chunk_prep",
    )(q2, k2, v2, gcol, gcum, bcol)

    seq_spec = pl.BlockSpec((1, Tp, hp * DK_A), lambda b, h: (b, 0, h))
    state_spec = pl.BlockSpec((1, hp, DK_A, DV_A), lambda b, h: (b, h, 0, 0))
    o, S = pl.pallas_call(
        functools.partial(_gdn_state_kernel, n_chunks=nc),
        out_shape=(jax.ShapeDtypeStruct((B, Tp, H * DV_A), jnp.float32),
                   jax.ShapeDtypeStruct((B, H, DK_A, DV_A), jnp.float32)),
        grid=(B, H // hp),
        in_specs=[seq_spec, seq_spec, seq_spec, seq_spec,
                  pl.BlockSpec((1, hp, Tp, L), lambda b, h: (b, h, 0, 0)),
                  pl.BlockSpec((1, hp, Tp, 1), lambda b, h: (b, h, 0, 0)),
                  pl.BlockSpec((1, hp, nc, L), lambda b, h: (b, h, 0, 0)), state_spec],
        out_specs=(seq_spec, state_spec),
        compiler_params=pltpu.CompilerParams(dimension_semantics=("parallel", "parallel"),
                                             vmem_limit_bytes=V7X_VMEM_LIMIT_BYTES),
        name="gdn_state",
    )(q2, k2, u, w, attn, gcol, gcum, S0)
    return o.reshape(B, Tp, H, DV_A)[:, :T], S


def gdn_mixer(qkv_raw, z, a, b, conv_prev, S0, conv_w, a_log, dt_bias, norm_w):
    B, T, _ = qkv_raw.shape
    xp = jnp.concatenate([conv_prev.astype(qkv_raw.dtype), qkv_raw], axis=1)
    conv = sum(xp[:, j:j + T] * conv_w[j] for j in range(CONV_W))
    qkv = jax.nn.silu(conv.astype(jnp.float32)).reshape(B, T, 3, H_A, DK_A)
    q = l2_normalize(qkv[:, :, 0]) * DK_A ** -0.5
    k = l2_normalize(qkv[:, :, 1])
    v = qkv[:, :, 2]
    beta = jax.nn.sigmoid(b.astype(jnp.float32))
    g = -jnp.exp(a_log.astype(jnp.float32)) * jax.nn.softplus(a.astype(jnp.float32) + dt_bias.astype(jnp.float32))
    o, S = gdn_recurrence(q, k, v, g, beta, S0.astype(jnp.float32))
    o = head_rms(o, norm_w) * jax.nn.silu(z.astype(jnp.float32).reshape(B, T, H_A, DV_A))
    return o.reshape(B, T, W_A).astype(qkv_raw.dtype), S, xp[:, T:]


def mlstm_recurrence(q, k, v, ig, lf, C0, n0, m0):
    T = q.shape[1]
    L = MLSTM_CHUNK
    n_pad = (-T) % L
    q, k, v, lf = [to_chunks(pad_time(t, n_pad), L) for t in (q, k, v, lf)]
    ig = to_chunks(pad_time(ig, n_pad, NEG), L)
    incl = jnp.tril(jnp.ones((L, L), bool))

    def step(carry, inp):
        C, n, m = carry
        qi, ki, vi, ii, fi = inp
        b = jnp.cumsum(fi, axis=-1)
        dmat = jnp.where(incl, b[..., :, None] - b[..., None, :] + ii[..., None, :], NEG)
        inter = b + m[..., None]
        m_tok = jnp.maximum(jnp.max(dmat, axis=-1), inter)
        s = jnp.einsum('bhid,bhjd->bhij', qi, ki) * jnp.exp(dmat - m_tok[..., None])
        w_inter = jnp.exp(inter - m_tok)
        num = w_inter[..., None] * jnp.einsum('bhid,bhde->bhie', qi, C) + jnp.einsum('bhij,bhje->bhie', s, vi)
        den = w_inter * jnp.einsum('bhid,bhd->bhi', qi, n) + jnp.sum(s, axis=-1)
        h = num / jnp.maximum(jnp.abs(den), jnp.exp(-m_tok))[..., None]
        b_last = b[..., -1]
        wk_log = b_last[..., None] - b + ii
        m_new = jnp.maximum(b_last + m, jnp.max(wk_log, axis=-1))
        wk = jnp.exp(wk_log - m_new[..., None])
        dec = jnp.exp(b_last + m - m_new)
        C = dec[..., None, None] * C + jnp.einsum('bhj,bhjd,bhje->bhde', wk, ki, vi)
        n = dec[..., None] * n + jnp.einsum('bhj,bhjd->bhd', wk, ki)
        return (C, n, m_new), h

    (C, n, m), h = lax.scan(step, (C0, n0, m0), (q, k, v, ig, lf))
    return from_chunks(h)[:, :T], C, n, m


def mlstm_mixer(q_raw, k_raw, v_raw, if_raw, o_raw, C0, n0, m0, b_if, norm_w):
    B, T, _ = q_raw.shape
    f32 = jnp.float32
    q = q_raw.astype(f32).reshape(B, T, H_C, DK_C) * DK_C ** -0.5
    k = k_raw.astype(f32).reshape(B, T, H_C, DK_C)
    v = v_raw.astype(f32).reshape(B, T, H_C, DV_C)
    pre = softcap(if_raw.astype(f32).reshape(B, T, 2, H_C) + b_if.astype(f32), GATE_CAP)
    ig = pre[:, :, 0]
    lf = jax.nn.log_sigmoid(pre[:, :, 1])
    h, C, n, m = mlstm_recurrence(q, k, v, ig, lf, C0.astype(f32), n0.astype(f32), m0.astype(f32))
    h = head_rms(h, norm_w).reshape(B, T, W_C) * jax.nn.sigmoid(o_raw.astype(f32))
    return h.astype(q_raw.dtype), C, n, m


def cmp_pool(rows, pe):
    lead = rows.shape[:-4]
    n = rows.shape[-4] // CMP_BLK
    blocks = rows.reshape(lead + (n, CMP_BLK, 2, KV_B, DH_B))
    return jnp.mean(blocks * jnp.transpose(pe, (1, 0, 2))[:, :, None, :], axis=-4)


def nsa_compressed(q, qpos, pooled, w, table, token_major=False):
    nb = pooled.shape[1]
    summ = jnp.einsum('bncgd,cde->bncge', pooled, w)
    end_pos = (jnp.arange(nb) + 1) * CMP_BLK - 1
    dist = qpos[:, None] - end_pos[None, :]
    s = (jnp.einsum('bgrqd,bngd->bgrqn', q, summ[:, :, 0]).astype(jnp.float32) * DH_B ** -0.5
         + rel_bias_heads(dist, table))
    p = masked_softmax(s, dist >= 0)
    o = jnp.einsum('bgrqn,bngd->bqgrd' if token_major else 'bgrqn,bngd->bgrqd', p.astype(summ.dtype), summ[:, :, 1])
    return o, p


def nsa_select(p_cmp, qpos, total_len):
    n_slc = -(-total_len // SLC_BLK)
    ratio = SLC_BLK // CMP_BLK
    imp = jnp.sum(p_cmp, axis=2)
    imp = jnp.pad(imp, ((0, 0), (0, 0), (0, 0), (0, n_slc * ratio - imp.shape[-1])))
    imp = imp.reshape(imp.shape[:-1] + (n_slc, ratio)).sum(-1)
    blk = jnp.arange(n_slc)[None, :]
    cur = (qpos // SLC_BLK)[:, None]
    valid = blk <= cur
    forced = (blk == 0) | (blk == cur) | (blk == cur - 1)
    score = jnp.where(valid & forced, SEL_FORCE, jnp.where(valid, imp, -SEL_FORCE))
    _, idx = lax.top_k(score, min(N_SEL, n_slc))
    return idx


def nsa_selected(q, qpos, idx, fetch, table):
    B, _, _, Tq, _ = q.shape
    n_sel = idx.shape[-1]
    qb = math.gcd(Tq, SLC_QBLK)
    nq = Tq // qb
    q_ch = jnp.moveaxis(q.reshape(B, KV_B, GRP_B, nq, qb, DH_B), 3, 0)
    pos_ch = qpos.reshape(nq, qb)
    idx_ch = jnp.moveaxis(idx.reshape(B, KV_B, nq, qb, n_sel), 2, 0)
    table_g = table.reshape(REL_BUCKETS, KV_B, GRP_B)
    gi = jnp.arange(KV_B)[None, :, None, None]
    offs = jnp.arange(SLC_BLK)

    def one_block(args):
        qc, pc, ic = args
        kpos = (ic[..., None] * SLC_BLK + offs).reshape(B, KV_B, qb, n_sel * SLC_BLK)
        kc, vc = fetch(kpos)
        dist = pc[None, None, :, None] - kpos
        bias = jnp.moveaxis(table_g[rel_bucket(dist), gi].astype(jnp.float32), -1, 2)
        s = jnp.einsum('bgrqd,bgqkd->bgrqk', qc, kc).astype(jnp.float32) * DH_B ** -0.5 + bias
        p = masked_softmax(s, (dist >= 0)[:, :, None])
        return jnp.einsum('bgrqk,bgqkd->bgrqd', p.astype(vc.dtype), vc)

    o = lax.map(one_block, (q_ch, pos_ch, idx_ch))
    return jnp.moveaxis(o, 0, 3).reshape(B, KV_B, GRP_B, Tq, DH_B)


NSA_TILE = 256
NSA_SUB_ROWS = 128
SLC_SHIFT = SLC_BLK.bit_length() - 1
assert 1 << SLC_SHIFT == SLC_BLK


def _nsa_flash_kernel(cfar_ref, q_ref, k_ref, v_ref, *rest, n_slc, window):
    if window:
        band_ref, o_ref, m_ref, l_ref, acc_ref = rest
    else:
        sel_ref, band_ref, o_ref, m_ref, l_ref, acc_ref = rest
        selt = sel_ref[0, 0].astype(jnp.bfloat16)
    g = pl.program_id(1)
    qi = pl.program_id(2)
    tq = tk = NSA_TILE
    bf16 = jnp.bfloat16
    f32 = jnp.float32
    qs = [q_ref[0, :, r * DH_B:(r + 1) * DH_B].astype(bf16) for r in range(GRP_B)]

    m_ref[...] = jnp.full(m_ref.shape, NEG, f32)
    l_ref[...] = jnp.zeros(l_ref.shape, f32)
    acc_ref[...] = jnp.zeros(acc_ref.shape, f32)

    def process(j, bias_of_head, triangle):
        k0 = pl.multiple_of(j * tk, tk)
        kt = k_ref[0, pl.ds(k0, tk), :].astype(bf16)
        vt = v_ref[0, pl.ds(k0, tk), :].astype(bf16)
        mask = None
        if triangle is not None:
            row = lax.broadcasted_iota(jnp.int32, (tq, tk), 0)
            col = lax.broadcasted_iota(jnp.int32, (tq, tk), 1)
            mask = row >= col if triangle == "causal" else col > row
        if not window:
            key_blk = jnp.right_shift(k0 + lax.broadcasted_iota(jnp.int32, (n_slc, tk), 1), SLC_SHIFT)
            expand = (key_blk == lax.broadcasted_iota(jnp.int32, (n_slc, tk), 0)).astype(bf16)
            chosen = jnp.dot(selt, expand, preferred_element_type=f32) > 0.5
            mask = chosen if mask is None else jnp.logical_and(mask, chosen)
        for r in range(GRP_B):
            for r0 in range(0, tq, NSA_SUB_ROWS):
                rows = slice(r0, r0 + NSA_SUB_ROWS)
                s = lax.dot_general(qs[r][rows], kt, (((1,), (1,)), ((), ())), preferred_element_type=f32)
                s = s * DH_B ** -0.5 + bias_of_head(r, rows)
                if mask is not None:
                    s = jnp.where(mask[rows], s, NEG)
                m_prev = m_ref[r, rows]
                m_next = jnp.maximum(m_prev, jnp.max(s, axis=-1, keepdims=True))
                p = jnp.exp(s - m_next)
                if mask is not None:
                    p = jnp.where(mask[rows], p, 0.0)
                alpha = jnp.exp(m_prev - m_next)
                l_ref[r, rows] = alpha * l_ref[r, rows] + jnp.sum(p, axis=-1, keepdims=True)
                acc_ref[r, rows] = alpha * acc_ref[r, rows] + jnp.dot(p.astype(bf16), vt,
                                                                      preferred_element_type=f32)
                m_ref[r, rows] = m_next

    def far_bias(r, rows):
        return cfar_ref[g * GRP_B + r]

    if window:
        @pl.when(qi >= 2)
        def _():
            process(qi - 2, far_bias, "before")
    else:
        def far_tile(j, carry):
            process(j, far_bias, None)
            return carry

        lax.fori_loop(0, jnp.maximum(qi - 1, 0), far_tile, 0)

    @pl.when(qi >= 1)
    def _():
        process(qi - 1, lambda r, rows: band_ref[1, 0, r, rows], None)

    process(qi, lambda r, rows: band_ref[0, 0, r, rows], "causal")

    for r in range(GRP_B):
        o_ref[0, :, r * DH_B:(r + 1) * DH_B] = acc_ref[r] / l_ref[r]


def nsa_flash_contiguous(b_q, b_kv, table, branch, idx=None):
    B, T, _ = b_q.shape
    tq = tk = NSA_TILE
    window = idx is None
    assert T % tq == 0 and tk >= REL_MAX_DIST and tk % SLC_BLK == 0 and WINDOW == 2 * tk
    n_slc = -(-T // SLC_BLK)
    rel = jnp.arange(tq)[:, None] - jnp.arange(tk)[None, :]
    dist = jnp.stack([rel, rel + tk])
    band = table[rel_bucket(dist)].astype(jnp.float32).reshape(2, tq, tk, KV_B, GRP_B).transpose(0, 3, 4, 1, 2)
    cfar = table[rel_bucket(jnp.int32(2 * tk))].astype(jnp.float32)
    k_col = (branch * 2 + 0) * KV_B
    v_col = (branch * 2 + 1) * KV_B
    grp_w = GRP_B * DH_B
    sel_args, sel_specs = [], []
    if not window:
        sel_args = [jnp.any(idx[..., None] == jnp.arange(n_slc, dtype=idx.dtype), axis=-2).astype(jnp.float32)]
        sel_specs = [pl.BlockSpec((1, 1, tq, n_slc), lambda b, g, i: (b, g, i, 0))]
    return pl.pallas_call(
        functools.partial(_nsa_flash_kernel, n_slc=n_slc, window=window),
        out_shape=jax.ShapeDtypeStruct((B, T, W_B), jnp.float32),
        grid=(B, KV_B, T // tq),
        in_specs=[pl.BlockSpec(memory_space=pltpu.SMEM),
                  pl.BlockSpec((1, tq, grp_w), lambda b, g, i: (b, i, g)),
                  pl.BlockSpec((1, T, DH_B), lambda b, g, i: (b, 0, k_col + g)),
                  pl.BlockSpec((1, T, DH_B), lambda b, g, i: (b, 0, v_col + g))]
                 + sel_specs
                 + [pl.BlockSpec((2, 1, GRP_B, tq, tk), lambda b, g, i: (0, g, 0, 0, 0))],
        out_specs=pl.BlockSpec((1, tq, grp_w), lambda b, g, i: (b, i, g)),
        scratch_shapes=[pltpu.VMEM((GRP_B, tq, 1), jnp.float32),
                        pltpu.VMEM((GRP_B, tq, 1), jnp.float32),
                        pltpu.VMEM((GRP_B, tq, DH_B), jnp.float32)],
        compiler_params=pltpu.CompilerParams(
            dimension_semantics=("parallel", "parallel", "arbitrary"),
            vmem_limit_bytes=V7X_VMEM_LIMIT_BYTES),
        name="nsa_window" if window else "nsa_selected",
    )(cfar, b_q, b_kv, b_kv, *sel_args, band)


def nsa_window(q, qpos, kb, kpos, table):
    dist = qpos[:, :, None] - kpos[:, None, :]
    mask = (dist >= 0) & (dist < WINDOW) & (kpos[:, None, :] >= 0)
    s = (jnp.einsum('bgrnqd,bnkgd->bgrnqk', q, kb[:, :, :, 0]).astype(jnp.float32) * DH_B ** -0.5
         + rel_bias_heads(dist, table))
    p = masked_softmax(s, mask)
    return jnp.einsum('bgrnqk,bnkgd->bgrnqd', p.astype(kb.dtype), kb[:, :, :, 1])


def nsa_mixer(q_raw, qpos, gate_raw, pooled_cmp, total_len, fetch, win_blocks, win_kpos, cmp_w, table,
              contiguous_kv=None):
    B, Tq, _ = q_raw.shape
    q = q_raw.reshape(B, Tq, KV_B, GRP_B, DH_B).transpose(0, 2, 3, 1, 4)
    o_cmp, p_cmp = nsa_compressed(q, qpos, pooled_cmp, cmp_w, table, token_major=contiguous_kv is not None)
    idx = nsa_select(p_cmp, qpos, total_len)
    if contiguous_kv is not None:
        o_slc = nsa_flash_contiguous(q_raw, contiguous_kv, table, 1, idx).reshape(B, Tq, KV_B, GRP_B, DH_B)
        o_win = nsa_flash_contiguous(q_raw, contiguous_kv, table, 2).reshape(B, Tq, KV_B, GRP_B, DH_B)
        g = jax.nn.sigmoid(gate_raw.astype(jnp.float32).reshape(B, Tq, 3, KV_B, GRP_B, 1))
        o = g[:, :, 0] * o_cmp + g[:, :, 1] * o_slc + g[:, :, 2] * o_win
        return o.reshape(B, Tq, W_B).astype(q_raw.dtype)
    o_slc = fetch(q, qpos, idx)
    nq = win_blocks.shape[1]
    o_win = nsa_window(q.reshape(B, KV_B, GRP_B, nq, Tq // nq, DH_B), qpos.reshape(nq, Tq // nq),
                       win_blocks, win_kpos, table).reshape(B, KV_B, GRP_B, Tq, DH_B)
    g = jax.nn.sigmoid(gate_raw.astype(jnp.float32).reshape(B, Tq, 3, KV_B, GRP_B)).transpose(2, 0, 3, 4, 1)
    o = g[0][..., None] * o_cmp + g[1][..., None] * o_slc + g[2][..., None] * o_win
    return o.transpose(0, 3, 1, 2, 4).reshape(B, Tq, W_B).astype(q_raw.dtype)


def nsa_selected_paged(q, qpos, idx, pool_l, page_table, kv_new, table):
    B, _, _, Tq, _ = q.shape
    n_sel = idx.shape[-1]
    Tn = kv_new.shape[1]
    per_page = PAGE_SIZE // SLC_BLK
    n_past_blk = PAST_LEN // SLC_BLK
    assert PAGE_SIZE % SLC_BLK == 0 and PAST_LEN % SLC_BLK == 0
    kpos = idx[..., None] * SLC_BLK + jnp.arange(SLC_BLK)
    in_past = idx < n_past_blk
    blk = jnp.minimum(idx, n_past_blk - 1)
    bi = jnp.arange(B)[:, None, None, None]
    page = page_table[bi, blk // per_page]
    slices = pool_l.reshape(pool_l.shape[0] * per_page, SLC_BLK, 2, KV_B, DH_B)
    got = slices[page * per_page + blk % per_page]
    past_rows = jnp.einsum('bgtsrcgd->bgtsrcd', got).astype(kv_new.dtype)
    gi = jnp.arange(KV_B)[None, :, None, None, None]
    new_rows = kv_new[bi[..., None], jnp.clip(kpos - PAST_LEN, 0, Tn - 1), :, gi]
    rows = jnp.where(in_past[..., None, None, None], past_rows, new_rows)
    rows = rows.reshape(B, KV_B, Tq, n_sel * SLC_BLK, 2, DH_B)
    kc, vc = rows[..., 0, :], rows[..., 1, :]
    kpos = kpos.reshape(B, KV_B, Tq, n_sel * SLC_BLK)
    dist = qpos[None, None, :, None] - kpos
    table_g = table.reshape(REL_BUCKETS, KV_B, GRP_B)
    bias = jnp.moveaxis(table_g[rel_bucket(dist), jnp.arange(KV_B)[None, :, None, None]].astype(jnp.float32), -1, 2)
    s = jnp.einsum('bgrqd,bgqkd->bgrqk', q, kc).astype(jnp.float32) * DH_B ** -0.5 + bias
    p = masked_softmax(s, (dist >= 0)[:, :, None])
    return jnp.einsum('bgrqk,bgqkd->bgrqd', p.astype(vc.dtype), vc)


def contiguous_fetch(kv):
    bi = jnp.arange(kv.shape[0])[:, None, None, None]
    gi = jnp.arange(KV_B)[None, :, None, None]

    def fetch(kpos):
        rows = kv[bi, kpos, :, gi]
        return rows[..., 0, :], rows[..., 1, :]
    return fetch


def paged_fetch(pool, l, page_table, kv_new):
    B, Tn = kv_new.shape[:2]
    bi = jnp.arange(B)[:, None, None, None]
    gi = jnp.arange(KV_B)[None, :, None, None]
    li = jnp.full((1, 1, 1, 1), l, jnp.int32)

    def fetch(kpos):
        in_past = kpos < PAST_LEN
        pc = jnp.minimum(kpos, PAST_LEN - 1)
        page = page_table[bi, pc // PAGE_SIZE]
        past_rows = pool[li, page, pc % PAGE_SIZE, :, gi].astype(kv_new.dtype)
        new_rows = kv_new[bi, jnp.clip(kpos - PAST_LEN, 0, Tn - 1), :, gi]
        rows = jnp.where(in_past[..., None, None], past_rows, new_rows)
        return rows[..., 0, :], rows[..., 1, :]
    return fetch


MOE_ROWS = 512
MOE_UP_COLS = 512
MOE_DOWN_COLS = 1024


def _moe_up_kernel(blk_ref, ocol_ref, exp_ref, wcol_ref, flag_ref, x_ref, wg_ref, wu_ref, bg_ref, bu_ref,
                   h_ref, wg_bf, wu_bf):
    flag = flag_ref[pl.program_id(0)]

    @pl.when(flag >= 2)
    def _():
        @pl.when(flag == 3)
        def _():
            wg_bf[...] = wg_ref[...].astype(jnp.bfloat16)
            wu_bf[...] = wu_ref[...].astype(jnp.bfloat16)
        x = x_ref[...]
        gate = jnp.minimum(jnp.dot(x, wg_bf[...], preferred_element_type=jnp.float32) + bg_ref[...], SWIGLU_LIMIT)
        up = jnp.clip(jnp.dot(x, wu_bf[...], preferred_element_type=jnp.float32) + bu_ref[...],
                      -SWIGLU_LIMIT, SWIGLU_LIMIT)
        h_ref[...] = ((up + 1.0) * gate * jax.nn.sigmoid(SWIGLU_ALPHA * gate)).astype(h_ref.dtype)

    @pl.when(flag < 2)
    def _():
        h_ref[...] = jnp.zeros(h_ref.shape, h_ref.dtype)


def _moe_down_kernel(blk_ref, ocol_ref, exp_ref, wcol_ref, flag_ref, h_ref, wd_ref, bd_ref, y_ref, wd_bf):
    flag = flag_ref[pl.program_id(0)]

    @pl.when(flag >= 2)
    def _():
        @pl.when(flag == 3)
        def _():
            wd_bf[...] = wd_ref[...].astype(jnp.bfloat16)
        y_ref[...] = jnp.dot(h_ref[...], wd_bf[...], preferred_element_type=jnp.float32) + bd_ref[...]

    @pl.when(flag < 2)
    def _():
        y_ref[...] = jnp.zeros(y_ref.shape, y_ref.dtype)


def _moe_expert_matmul(body, nblk_e, first_blk, n_blocks, n_col, tn, rows, weights, biases, l, out_dtype, name):
    tm = MOE_ROWS
    K = rows[0].shape[1]
    N = n_col * tn
    n_items = n_blocks * n_col
    i = jnp.arange(n_items, dtype=jnp.int32)
    total_blocks = jnp.sum(nblk_e)
    total = total_blocks * n_col
    valid = i < total
    iv = jnp.minimum(i, total - 1)
    e = jnp.minimum(jnp.searchsorted(jnp.cumsum(nblk_e) * n_col, iv, side='right'), N_EXPERTS - 1).astype(jnp.int32)
    j = iv - first_blk[e] * n_col
    nb = jnp.maximum(nblk_e[e], 1)
    wcol = j // nb
    mb = j % nb
    spare = i - total
    blk = jnp.where(valid, first_blk[e] + mb, total_blocks + spare // n_col).astype(jnp.int32)
    ocol = jnp.where(valid, wcol, spare % n_col).astype(jnp.int32)
    flag = jnp.where(valid, 2 + (mb == 0), 0).astype(jnp.int32)
    wcol = wcol.astype(jnp.int32)

    w_spec = pl.BlockSpec((None, None, K, tn), lambda s, blk, ocol, e, wcol, flag: (l, e[s], 0, wcol[s]))
    b_spec = pl.BlockSpec((None, None, 1, tn), lambda s, blk, ocol, e, wcol, flag: (l, e[s], 0, wcol[s]))
    grid_spec = pltpu.PrefetchScalarGridSpec(
        num_scalar_prefetch=5,
        grid=(n_items,),
        in_specs=([pl.BlockSpec((tm, K), lambda s, blk, ocol, e, wcol, flag: (blk[s], 0))]
                  + [w_spec] * len(weights) + [b_spec] * len(biases)),
        out_specs=pl.BlockSpec((tm, tn), lambda s, blk, ocol, e, wcol, flag: (blk[s], ocol[s])),
        scratch_shapes=[pltpu.VMEM((K, tn), jnp.bfloat16)] * len(weights),
    )
    return pl.pallas_call(
        body,
        out_shape=jax.ShapeDtypeStruct((n_blocks * tm, N), out_dtype),
        grid_spec=grid_spec,
        compiler_params=pltpu.CompilerParams(dimension_semantics=("arbitrary",),
                                             vmem_limit_bytes=V7X_VMEM_LIMIT_BYTES),
        name=name,
    )(blk, ocol, e, wcol, flag, *rows, *weights,
      *[b.reshape(b.shape[0], b.shape[1], 1, b.shape[2]) for b in biases])


def moe_ffn(x, l, router_w, router_b, w_gate, b_gate, w_up, b_up, w_down, b_down):
    shape = x.shape
    xt = x.reshape(-1, D_MODEL)
    T = xt.shape[0]
    logits = jnp.dot(xt, router_w[l]).astype(jnp.float32) + router_b[l].astype(jnp.float32)
    top_val, top_exp = lax.top_k(logits, TOP_K)
    weights = jax.nn.softmax(top_val, axis=-1)
    n = T * TOP_K
    blk = MOE_ROWS
    flat_e = top_exp.reshape(-1)
    order = jnp.argsort(flat_e)
    e_sorted = flat_e[order]
    counts = jnp.zeros((N_EXPERTS,), jnp.int32).at[flat_e].add(1)
    nblk_e = (counts + blk - 1) // blk
    start = jnp.cumsum(counts) - counts
    first_blk = jnp.cumsum(nblk_e) - nblk_e
    dest = first_blk[e_sorted] * blk + jnp.arange(n, dtype=jnp.int32) - start[e_sorted]
    n_blocks = -(-(n + N_EXPERTS * (blk - 1)) // blk)
    slot_tok = jnp.full((n_blocks * blk,), T, jnp.int32).at[dest].set((order // TOP_K).astype(jnp.int32))
    slot_of = jnp.zeros((n,), jnp.int32).at[order].set(dest).reshape(T, TOP_K)
    x_ext = jnp.concatenate([xt.astype(jnp.bfloat16), jnp.zeros((1, D_MODEL), jnp.bfloat16)], axis=0)
    x_rows = x_ext[slot_tok]

    hb = _moe_expert_matmul(_moe_up_kernel, nblk_e, first_blk, n_blocks, D_FF // MOE_UP_COLS, MOE_UP_COLS,
                            [x_rows], [w_gate, w_up], [b_gate, b_up], l, jnp.bfloat16, "moe_gate_up")
    y_rows = _moe_expert_matmul(_moe_down_kernel, nblk_e, first_blk, n_blocks, D_MODEL // MOE_DOWN_COLS,
                                MOE_DOWN_COLS, [hb], [w_down], [b_down], l, jnp.float32, "moe_down")
    y = jnp.sum(y_rows[slot_of] * weights[..., None], axis=1)
    return y.astype(x.dtype).reshape(shape)


def token_mixers(h, l, qpos, prm, past):
    B, T, _ = h.shape
    f32 = jnp.float32
    proj = _dot2(h, prm['w_in'][l])
    (a_qkv, a_z, a_a, a_b, b_q, b_kv, b_g, c_q, c_k, c_v, c_if, c_o, merge_g) = jnp.split(proj, IN_OFFSETS, axis=-1)
    if past is None:
        conv_prev = jnp.zeros((B, CONV_W - 1, 3 * W_A), h.dtype)
        S0 = jnp.zeros((B, H_A, DK_A, DV_A), f32)
        C0 = jnp.zeros((B, H_C, DK_C, DV_C), f32)
        n0 = jnp.zeros((B, H_C, DK_C), f32)
        m0 = jnp.full((B, H_C), NEG, f32)
    else:
        conv_prev = past['gdn_conv'][l]
        S0 = past['gdn_S'][l]
        C0 = past['mlstm_C'][l]
        n0 = past['mlstm_n'][l]
        m0 = past['mlstm_m'][l]

    o_a, S_a, conv_a = gdn_mixer(a_qkv, a_z, a_a, a_b, conv_prev, S0, prm['gdn_conv_w'][l],
                                 prm['gdn_a_log'][l], prm['gdn_dt_bias'][l], prm['gdn_norm_w'][l])

    kv_new = b_kv.reshape(B, T, 3, 2, KV_B, DH_B)
    kv_cmp, kv_slc, kv_win = kv_new[:, :, 0], kv_new[:, :, 1], kv_new[:, :, 2]
    cmp_pe = prm['nsa_cmp_pe'][l]
    if past is None:
        total_len = T
        pooled_cmp = cmp_pool(kv_cmp[:, :T // CMP_BLK * CMP_BLK], cmp_pe)
        fetch = win_blocks = win_kpos = None
        win_all = kv_win
    else:
        assert PAGE_SIZE % CMP_BLK == 0 and PAST_LEN % PAGE_SIZE == 0
        total_len = PAST_LEN + T
        page_pooled = cmp_pool(past['kv_cmp'][l].astype(kv_cmp.dtype), cmp_pe)
        pooled_cmp = page_pooled[past['page_table']].reshape(B, PAST_LEN // CMP_BLK, 2, KV_B, DH_B)
        n_new_blocks = total_len // CMP_BLK - PAST_LEN // CMP_BLK
        if n_new_blocks:
            pooled_cmp = jnp.concatenate([pooled_cmp, cmp_pool(kv_cmp[:, :n_new_blocks * CMP_BLK], cmp_pe)], axis=1)
        fetch = functools.partial(nsa_selected_paged, pool_l=past['kv_slc'][l], page_table=past['page_table'],
                                  kv_new=kv_slc, table=prm['rel_bias_table'])
        buf = past['kv_win'][l].astype(kv_win.dtype)
        wb = buf.shape[1]
        win_all = jnp.concatenate([buf, kv_win], axis=1)
        win_blocks = win_all[:, None]
        win_kpos = (PAST_LEN - wb + jnp.arange(wb + T))[None, :]
    o_b = nsa_mixer(b_q, qpos, b_g, pooled_cmp, total_len, fetch, win_blocks, win_kpos,
                    prm['nsa_cmp_w'][l], prm['rel_bias_table'],
                    contiguous_kv=b_kv if past is None else None)
    new_win = win_all[:, -min(WINDOW, win_all.shape[1]):]

    o_c, C, n, m = mlstm_mixer(c_q, c_k, c_v, c_if, c_o, C0, n0, m0, prm['mlstm_b_if'][l], prm['mlstm_norm_w'][l])

    merged = merge_up(o_a.reshape(B * T, W_A), o_b.reshape(B * T, W_B), o_c.reshape(B * T, W_C),
                      merge_g.astype(f32).reshape(B * T, 3 * D_MODEL),
                      prm['w_up_a'][l], prm['w_up_b'][l], prm['w_up_c'][l])
    out = _dot2(merged.reshape(B, T, D_MODEL), prm['w_out'][l])
    return out, (kv_cmp, kv_slc, new_win, S_a, conv_a, C, n, m)


def kernel(x_prompt, x_sample, cache_kv_cmp, cache_kv_slc, cache_kv_win, state_gdn_S, state_gdn_conv,
           state_mlstm_C, state_mlstm_n, state_mlstm_m, page_table,
           norm_mix, norm_ffn, norm_final, w_in, gdn_conv_w, gdn_a_log, gdn_dt_bias, gdn_norm_w,
           nsa_cmp_pe, nsa_cmp_w, rel_bias_table, mlstm_b_if, mlstm_norm_w,
           w_up_a, w_up_b, w_up_c, w_out, router_w, router_b,
           exp_w_gate, exp_b_gate, exp_w_up, exp_b_up, exp_w_down, exp_b_down):
    prm = {'w_in': w_in, 'gdn_conv_w': gdn_conv_w, 'gdn_a_log': gdn_a_log, 'gdn_dt_bias': gdn_dt_bias,
           'gdn_norm_w': gdn_norm_w, 'nsa_cmp_pe': nsa_cmp_pe, 'nsa_cmp_w': nsa_cmp_w,
           'rel_bias_table': rel_bias_table, 'mlstm_b_if': mlstm_b_if, 'mlstm_norm_w': mlstm_norm_w,
           'w_up_a': w_up_a, 'w_up_b': w_up_b, 'w_up_c': w_up_c, 'w_out': w_out}
    past = {'kv_cmp': cache_kv_cmp, 'kv_slc': cache_kv_slc, 'kv_win': cache_kv_win, 'gdn_S': state_gdn_S,
            'gdn_conv': state_gdn_conv, 'mlstm_C': state_mlstm_C, 'mlstm_n': state_mlstm_n,
            'mlstm_m': state_mlstm_m, 'page_table': page_table}

    qpos_p = jnp.arange(SEQ, dtype=jnp.int32)
    qpos_s = PAST_LEN + jnp.arange(DEC_SEQ, dtype=jnp.int32)
    n_p = BATCH * SEQ
    xp, xs = x_prompt, x_sample
    states_p, states_s = [], []
    for l in range(DEPTH):
        mix_p, st_p = token_mixers(rms_norm(xp, norm_mix[l]), l, qpos_p, prm, None)
        mix_s, st_s = token_mixers(rms_norm(xs, norm_mix[l]), l, qpos_s, prm, past)
        xp = xp + mix_p
        xs = xs + mix_s
        h_all = jnp.concatenate([rms_norm(xp, norm_ffn[l]).reshape(n_p, D_MODEL),
                                 rms_norm(xs, norm_ffn[l]).reshape(-1, D_MODEL)], axis=0)
        y_all = moe_ffn(h_all, l, router_w, router_b, exp_w_gate, exp_b_gate,
                        exp_w_up, exp_b_up, exp_w_down, exp_b_down)
        xp = xp + y_all[:n_p].reshape(xp.shape)
        xs = xs + y_all[n_p:].reshape(xs.shape)
        states_p.append(st_p)
        states_s.append(st_s)
    y_prompt = rms_norm(xp, norm_final)
    y_sample = rms_norm(xs, norm_final)
    kvc_p, kvs_p, win_p, gS_p, gconv_p, mC_p, mn_p, mm_p = [jnp.stack(s, axis=0) for s in zip(*states_p)]
    kvc_s, kvs_s, win_s, gS_s, gconv_s, mC_s, mn_s, mm_s = [jnp.stack(s, axis=0) for s in zip(*states_s)]
    return (y_prompt, y_sample, kvc_p, kvc_s, kvs_p, kvs_s, win_p, win_s, gS_p, gS_s,
            gconv_p, gconv_s, mC_p, mC_s, mn_p, mn_s, mm_p, mm_s)
```

```python
import functools
import math

import jax
import jax.numpy as jnp
import numpy as np
from jax import lax
from jax.experimental import pallas as pl
from jax.experimental.pallas import tpu as pltpu

D_MODEL = 4096
BATCH = 4
SEQ = 2048
DEPTH = 2
DEC_BATCH = 8
DEC_SEQ = 4
PAST_LEN = 16384
PAGE_SIZE = 128

H_A = D_MODEL // 512
DK_A = 128
DV_A = 128
W_A = H_A * DV_A
CONV_W = 4
GDN_CHUNK = 64
H_B = D_MODEL // 256
KV_B = 4
GRP_B = H_B // KV_B
DH_B = 128
W_B = H_B * DH_B
CMP_BLK = 32
SLC_BLK = 64
N_SEL = 16
WINDOW = 512
WIN_QBLK = 128
SLC_QBLK = 16
SEL_FORCE = 1e9
H_C = D_MODEL // 512
DK_C = 64
DV_C = 128
W_C = H_C * DV_C
MLSTM_CHUNK = 64
GATE_CAP = 15.0
REL_BUCKETS = 32
REL_EXACT = REL_BUCKETS // 2
REL_MAX_DIST = 128
N_EXPERTS = 32
TOP_K = 4
D_FF = D_MODEL // 2
SWIGLU_ALPHA = 1.702
SWIGLU_LIMIT = 7.0
MOE_BLOCK = 128
MOE_BLOCK_SMALL = 8

RMS_EPS = 1e-6
NEG = -1e30

IN_SIZES = (3 * W_A, W_A, H_A, H_A,
            W_B, 6 * KV_B * DH_B, 3 * H_B,
            H_C * DK_C, H_C * DK_C, W_C, 2 * H_C, W_C,
            3 * D_MODEL)
IN_OFFSETS = tuple(int(c) for c in np.cumsum(IN_SIZES)[:-1])
IN_COLS = int(sum(IN_SIZES))

V7X_VMEM_LIMIT_BYTES = 56 * 1024 * 1024


DENSE_ROWS = 1024
DENSE_COLS = 512


def _mm_kernel(x_ref, w_ref, o_ref, w_bf):
    @pl.when(pl.program_id(1) == 0)
    def _():
        w_bf[...] = w_ref[...].astype(jnp.bfloat16)

    o_ref[...] = jnp.dot(x_ref[...], w_bf[...], preferred_element_type=jnp.float32)


def _matmul(x, w):
    M, K = x.shape
    N = w.shape[1]
    tm = min(DENSE_ROWS, M)
    tn = min(DENSE_COLS, N)
    assert M % tm == 0
    return pl.pallas_call(
        _mm_kernel,
        out_shape=jax.ShapeDtypeStruct((M, N), jnp.float32),
        grid=(pl.cdiv(N, tn), M // tm),
        in_specs=[pl.BlockSpec((tm, K), lambda j, i: (i, 0)),
                  pl.BlockSpec((K, tn), lambda j, i: (0, j))],
        out_specs=pl.BlockSpec((tm, tn), lambda j, i: (i, j)),
        scratch_shapes=[pltpu.VMEM((K, tn), jnp.bfloat16)],
        compiler_params=pltpu.CompilerParams(
            dimension_semantics=("parallel", "arbitrary"),
            vmem_limit_bytes=V7X_VMEM_LIMIT_BYTES),
        name="dense_matmul",
    )(x, w)


MERGE_ROWS = 512
MERGE_COLS = 512


def _merge_up_kernel(xa_ref, xb_ref, xc_ref, ga_ref, gb_ref, gc_ref, wa_ref, wb_ref, wc_ref, o_ref,
                     wa_bf, wb_bf, wc_bf):
    @pl.when(pl.program_id(1) == 0)
    def _():
        wa_bf[...] = wa_ref[...].astype(jnp.bfloat16)
        wb_bf[...] = wb_ref[...].astype(jnp.bfloat16)
        wc_bf[...] = wc_ref[...].astype(jnp.bfloat16)

    f32 = jnp.float32
    merged = (jax.nn.sigmoid(ga_ref[...]) * jnp.dot(xa_ref[...], wa_bf[...], preferred_element_type=f32)
              + jax.nn.sigmoid(gb_ref[...]) * jnp.dot(xb_ref[...], wb_bf[...], preferred_element_type=f32)
              + jax.nn.sigmoid(gc_ref[...]) * jnp.dot(xc_ref[...], wc_bf[...], preferred_element_type=f32))
    o_ref[...] = merged.astype(o_ref.dtype)


def merge_up(o_a, o_b, o_c, merge_g, w_a, w_b, w_c):
    M = o_a.shape[0]
    N = w_a.shape[1]
    tm = min(MERGE_ROWS, M)
    tn = MERGE_COLS
    assert M % tm == 0 and N % tn == 0
    n_col = N // tn
    xs = [o.astype(jnp.bfloat16) for o in (o_a, o_b, o_c)]
    ws = (w_a, w_b, w_c)

    def gate_spec(branch):
        return pl.BlockSpec((tm, tn), lambda j, i: (i, branch * n_col + j))

    return pl.pallas_call(
        _merge_up_kernel,
        out_shape=jax.ShapeDtypeStruct((M, N), jnp.bfloat16),
        grid=(n_col, M // tm),
        in_specs=([pl.BlockSpec((tm, x.shape[1]), lambda j, i: (i, 0)) for x in xs]
                  + [gate_spec(0), gate_spec(1), gate_spec(2)]
                  + [pl.BlockSpec((w.shape[0], tn), lambda j, i: (0, j)) for w in ws]),
        out_specs=pl.BlockSpec((tm, tn), lambda j, i: (i, j)),
        scratch_shapes=[pltpu.VMEM((w.shape[0], tn), jnp.bfloat16) for w in ws],
        compiler_params=pltpu.CompilerParams(
            dimension_semantics=("parallel", "arbitrary"),
            vmem_limit_bytes=V7X_VMEM_LIMIT_BYTES),
        name="merge_up",
    )(*xs, merge_g, merge_g, merge_g, *ws)


def _dot2(x, w):
    lead = x.shape[:-1]
    return _matmul(x.reshape(-1, x.shape[-1]).astype(jnp.bfloat16), w).reshape(lead + (w.shape[1],))


def rms_norm(x, g):
    xf = x.astype(jnp.float32)
    y = xf * lax.rsqrt(jnp.mean(xf * xf, axis=-1, keepdims=True) + RMS_EPS)
    return (y * g.astype(jnp.float32)).astype(x.dtype)


def head_rms(x, w):
    return x * lax.rsqrt(jnp.mean(x * x, axis=-1, keepdims=True) + RMS_EPS) * w.astype(jnp.float32)


def l2_normalize(x):
    return x * lax.rsqrt(jnp.sum(x * x, axis=-1, keepdims=True) + RMS_EPS)


def softcap(x, cap):
    return cap * jnp.tanh(x / cap)


def masked_softmax(s, mask):
    p = jax.nn.softmax(jnp.where(mask, s, NEG), axis=-1)
    return jnp.where(mask, p, 0.0)


def rel_bucket(dist):
    n = jnp.maximum(dist, 0)
    nf = jnp.maximum(n, 1).astype(jnp.float32)
    ratio = jnp.log(nf / REL_EXACT) / math.log(REL_MAX_DIST / REL_EXACT)
    large = REL_EXACT + (ratio * (REL_BUCKETS - REL_EXACT)).astype(jnp.int32)
    return jnp.where(n < REL_EXACT, n, jnp.minimum(large, REL_BUCKETS - 1))


def rel_bias_heads(dist, table):
    b = table[rel_bucket(dist)].astype(jnp.float32).reshape(dist.shape + (KV_B, GRP_B))
    nd = dist.ndim
    return jnp.transpose(b, (nd, nd + 1) + tuple(range(nd)))


def pad_time(x, n_pad, value=0.0):
    if n_pad == 0:
        return x
    widths = [(0, 0)] * x.ndim
    widths[1] = (0, n_pad)
    return jnp.pad(x, widths, constant_values=value)


def to_chunks(x, chunk):
    B, T = x.shape[:2]
    x = x.reshape((B, T // chunk, chunk) + x.shape[2:])
    return jnp.moveaxis(x, (1, 3), (0, 2))


def from_chunks(x):
    x = jnp.moveaxis(x, (0, 2), (1, 3))
    return x.reshape((x.shape[0], x.shape[1] * x.shape[2]) + x.shape[3:])


GDN_HEADS_PER_STEP = 2
GDN_CHUNKS_PER_STEP = 8


def _gdn_prep_kernel(q_ref, k_ref, v_ref, gcol_ref, grow_ref, bcol_ref, u_ref, w_ref, attn_ref, *, n_group):
    L = GDN_CHUNK
    f32 = jnp.float32
    bf16 = jnp.bfloat16
    row = lax.broadcasted_iota(jnp.int32, (L, L), 0)
    col = lax.broadcasted_iota(jnp.int32, (L, L), 1)
    incl = row >= col
    strict = row > col
    eye = (row == col).astype(f32)

    def mm(a, b):
        return jnp.dot(a.astype(bf16), b.astype(bf16), preferred_element_type=f32)

    def mm_nt(a, b):
        return lax.dot_general(a.astype(bf16), b.astype(bf16), (((1,), (1,)), ((), ())), preferred_element_type=f32)

    def mm_exact(a, b):
        return jnp.dot(a, b, preferred_element_type=f32, precision=lax.Precision.HIGHEST)

    chunks = range(n_group)
    rows = [slice(c * L, (c + 1) * L) for c in chunks]
    k = [k_ref[0, rows[c], :] for c in chunks]
    gc = [gcol_ref[0, 0, rows[c], :] for c in chunks]
    bc = [bcol_ref[0, 0, rows[c], :] for c in chunks]
    decay = [jnp.exp(jnp.where(incl, gc[c] - grow_ref[0, 0, c:c + 1, :], NEG)) for c in chunks]
    kb = [k[c] * bc[c] for c in chunks]
    lower = [jnp.where(strict, mm_nt(kb[c], k[c]) * decay[c], 0.0) for c in chunks]
    tinv = [eye - lower[c] for c in chunks]
    power = [mm_exact(lower[c], lower[c]) for c in chunks]
    n_levels = L.bit_length() - 2
    for level in range(n_levels):
        tinv = [tinv[c] + mm_exact(tinv[c], power[c]) for c in chunks]
        if level < n_levels - 1:
            power = [mm_exact(power[c], power[c]) for c in chunks]
    for c in chunks:
        u_ref[0, rows[c], :] = mm(tinv[c], v_ref[0, rows[c], :] * bc[c])
        w_ref[0, rows[c], :] = mm(tinv[c], kb[c] * jnp.exp(gc[c]))
        attn_ref[0, 0, rows[c], :] = jnp.where(incl, mm_nt(q_ref[0, rows[c], :], k[c]) * decay[c], 0.0)


def _gdn_state_kernel(q_ref, k_ref, u_ref, w_ref, attn_ref, gcol_ref, grow_ref, s0_ref, o_ref, s_ref, *, n_chunks):
    L = GDN_CHUNK
    hp = GDN_HEADS_PER_STEP
    f32 = jnp.float32
    bf16 = jnp.bfloat16

    def mm(a, b):
        return jnp.dot(a.astype(bf16), b.astype(bf16), preferred_element_type=f32)

    s_ref[...] = s0_ref[...]

    def chunk(c, carry):
        t0 = pl.multiple_of(c * L, L)
        for h in range(hp):
            lanes = slice(h * DK_A, (h + 1) * DK_A)
            q = q_ref[0, pl.ds(t0, L), lanes]
            k = k_ref[0, pl.ds(t0, L), lanes]
            gc = gcol_ref[0, h, pl.ds(t0, L), :]
            g_last = grow_ref[0, h, pl.ds(c, 1), :][:, L - 1:L]
            S = s_ref[0, h]
            v_new = u_ref[0, pl.ds(t0, L), lanes] - mm(w_ref[0, pl.ds(t0, L), lanes], S)
            o_ref[0, pl.ds(t0, L), lanes] = mm(q * jnp.exp(gc), S) + mm(attn_ref[0, h, pl.ds(t0, L), :], v_new)
            k_dec = (k * jnp.exp(g_last - gc)).astype(bf16)
            s_ref[0, h] = S * jnp.exp(g_last) + lax.dot_general(
                k_dec, v_new.astype(bf16), (((0,), (0,)), ((), ())), preferred_element_type=f32)
        return carry

    lax.fori_loop(0, n_chunks, chunk, 0)


def gdn_recurrence(q, k, v, g, beta, S0):
    B, T, H, _ = q.shape
    L = GDN_CHUNK
    hp = GDN_HEADS_PER_STEP
    assert DK_A == DV_A and H % hp == 0 and L & (L - 1) == 0
    n_pad = (-T) % L
    q, k, v, g, beta = [pad_time(t, n_pad) for t in (q, k, v, g, beta)]
    Tp = T + n_pad
    nc = Tp // L
    grp = math.gcd(nc, GDN_CHUNKS_PER_STEP)
    gcum = jnp.cumsum(g.reshape(B, nc, L, H), axis=2).transpose(0, 3, 1, 2)
    gcol = gcum.reshape(B, H, Tp, 1)
    bcol = beta.transpose(0, 2, 1).reshape(B, H, Tp, 1)
    q2, k2, v2 = q.reshape(B, Tp, H * DK_A), k.reshape(B, Tp, H * DK_A), v.reshape(B, Tp, H * DV_A)

    rows_spec = pl.BlockSpec((1, grp * L, DK_A), lambda b, h, c: (b, c, h))
    rcol_spec = pl.BlockSpec((1, 1, grp * L, 1), lambda b, h, c: (b, h, c, 0))
    u, w, attn = pl.pallas_call(
        functools.partial(_gdn_prep_kernel, n_group=grp),
        out_shape=(jax.ShapeDtypeStruct((B, Tp, H * DV_A), jnp.float32),
                   jax.ShapeDtypeStruct((B, Tp, H * DK_A), jnp.float32),
                   jax.ShapeDtypeStruct((B, H, Tp, L), jnp.float32)),
        grid=(B, H, nc // grp),
        in_specs=[rows_spec, rows_spec, rows_spec, rcol_spec,
                  pl.BlockSpec((1, 1, grp, L), lambda b, h, c: (b, h, c, 0)), rcol_spec],
        out_specs=(rows_spec, rows_spec, pl.BlockSpec((1, 1, grp * L, L), lambda b, h, c: (b, h, c, 0))),
        compiler_params=pltpu.CompilerParams(dimension_semantics=("parallel", "parallel", "parallel"),
                                             vmem_limit_bytes=V7X_VMEM_LIMIT_BYTES),
        name="gdn_chunk_prep",
    )(q2, k2, v2, gcol, gcum, bcol)

    seq_spec = pl.BlockSpec((1, Tp, hp * DK_A), lambda b, h: (b, 0, h))
    state_spec = pl.BlockSpec((1, hp, DK_A, DV_A), lambda b, h: (b, h, 0, 0))
    o, S = pl.pallas_call(
        functools.partial(_gdn_state_kernel, n_chunks=nc),
        out_shape=(jax.ShapeDtypeStruct((B, Tp, H * DV_A), jnp.float32),
                   jax.ShapeDtypeStruct((B, H, DK_A, DV_A), jnp.float32)),
        grid=(B, H // hp),
        in_specs=[seq_spec, seq_spec, seq_spec, seq_spec,
                  pl.BlockSpec((1, hp, Tp, L), lambda b, h: (b, h, 0, 0)),
                  pl.BlockSpec((1, hp, Tp, 1), lambda b, h: (b, h, 0, 0)),
                  pl.BlockSpec((1, hp, nc, L), lambda b, h: (b, h, 0, 0)), state_spec],
        out_specs=(seq_spec, state_spec),
        compiler_params=pltpu.CompilerParams(dimension_semantics=("parallel", "parallel"),
                                             vmem_limit_bytes=V7X_VMEM_LIMIT_BYTES),
        name="gdn_state",
    )(q2, k2, u, w, attn, gcol, gcum, S0)
    return o.reshape(B, Tp, H, DV_A)[:, :T], S


def gdn_mixer(qkv_raw, z, a, b, conv_prev, S0, conv_w, a_log, dt_bias, norm_w):
    B, T, _ = qkv_raw.shape
    xp = jnp.concatenate([conv_prev.astype(qkv_raw.dtype), qkv_raw], axis=1)
    conv = sum(xp[:, j:j + T] * conv_w[j] for j in range(CONV_W))
    qkv = jax.nn.silu(conv.astype(jnp.float32)).reshape(B, T, 3, H_A, DK_A)
    q = l2_normalize(qkv[:, :, 0]) * DK_A ** -0.5
    k = l2_normalize(qkv[:, :, 1])
    v = qkv[:, :, 2]
    beta = jax.nn.sigmoid(b.astype(jnp.float32))
    g = -jnp.exp(a_log.astype(jnp.float32)) * jax.nn.softplus(a.astype(jnp.float32) + dt_bias.astype(jnp.float32))
    o, S = gdn_recurrence(q, k, v, g, beta, S0.astype(jnp.float32))
    o = head_rms(o, norm_w) * jax.nn.silu(z.astype(jnp.float32).reshape(B, T, H_A, DV_A))
    return o.reshape(B, T, W_A).astype(qkv_raw.dtype), S, xp[:, T:]


def mlstm_recurrence(q, k, v, ig, lf, C0, n0, m0):
    T = q.shape[1]
    L = MLSTM_CHUNK
    n_pad = (-T) % L
    q, k, v, lf = [to_chunks(pad_time(t, n_pad), L) for t in (q, k, v, lf)]
    ig = to_chunks(pad_time(ig, n_pad, NEG), L)
    incl = jnp.tril(jnp.ones((L, L), bool))

    def step(carry, inp):
        C, n, m = carry
        qi, ki, vi, ii, fi = inp
        b = jnp.cumsum(fi, axis=-1)
        dmat = jnp.where(incl, b[..., :, None] - b[..., None, :] + ii[..., None, :], NEG)
        inter = b + m[..., None]
        m_tok = jnp.maximum(jnp.max(dmat, axis=-1), inter)
        s = jnp.einsum('bhid,bhjd->bhij', qi, ki) * jnp.exp(dmat - m_tok[..., None])
        w_inter = jnp.exp(inter - m_tok)
        num = w_inter[..., None] * jnp.einsum('bhid,bhde->bhie', qi, C) + jnp.einsum('bhij,bhje->bhie', s, vi)
        den = w_inter * jnp.einsum('bhid,bhd->bhi', qi, n) + jnp.sum(s, axis=-1)
        h = num / jnp.maximum(jnp.abs(den), jnp.exp(-m_tok))[..., None]
        b_last = b[..., -1]
        wk_log = b_last[..., None] - b + ii
        m_new = jnp.maximum(b_last + m, jnp.max(wk_log, axis=-1))
        wk = jnp.exp(wk_log - m_new[..., None])
        dec = jnp.exp(b_last + m - m_new)
        C = dec[..., None, None] * C + jnp.einsum('bhj,bhjd,bhje->bhde', wk, ki, vi)
        n = dec[..., None] * n + jnp.einsum('bhj,bhjd->bhd', wk, ki)
        return (C, n, m_new), h

    (C, n, m), h = lax.scan(step, (C0, n0, m0), (q, k, v, ig, lf))
    return from_chunks(h)[:, :T], C, n, m


def mlstm_mixer(q_raw, k_raw, v_raw, if_raw, o_raw, C0, n0, m0, b_if, norm_w):
    B, T, _ = q_raw.shape
    f32 = jnp.float32
    q = q_raw.astype(f32).reshape(B, T, H_C, DK_C) * DK_C ** -0.5
    k = k_raw.astype(f32).reshape(B, T, H_C, DK_C)
    v = v_raw.astype(f32).reshape(B, T, H_C, DV_C)
    pre = softcap(if_raw.astype(f32).reshape(B, T, 2, H_C) + b_if.astype(f32), GATE_CAP)
    ig = pre[:, :, 0]
    lf = jax.nn.log_sigmoid(pre[:, :, 1])
    h, C, n, m = mlstm_recurrence(q, k, v, ig, lf, C0.astype(f32), n0.astype(f32), m0.astype(f32))
    h = head_rms(h, norm_w).reshape(B, T, W_C) * jax.nn.sigmoid(o_raw.astype(f32))
    return h.astype(q_raw.dtype), C, n, m


def cmp_pool(rows, pe):
    lead = rows.shape[:-4]
    n = rows.shape[-4] // CMP_BLK
    blocks = rows.reshape(lead + (n, CMP_BLK, 2, KV_B, DH_B))
    return jnp.mean(blocks * jnp.transpose(pe, (1, 0, 2))[:, :, None, :], axis=-4)


def nsa_compressed(q, qpos, pooled, w, table, token_major=False):
    nb = pooled.shape[1]
    summ = jnp.einsum('bncgd,cde->bncge', pooled, w)
    end_pos = (jnp.arange(nb) + 1) * CMP_BLK - 1
    dist = qpos[:, None] - end_pos[None, :]
    s = (jnp.einsum('bgrqd,bngd->bgrqn', q, summ[:, :, 0]).astype(jnp.float32) * DH_B ** -0.5
         + rel_bias_heads(dist, table))
    p = masked_softmax(s, dist >= 0)
    o = jnp.einsum('bgrqn,bngd->bqgrd' if token_major else 'bgrqn,bngd->bgrqd', p.astype(summ.dtype), summ[:, :, 1])
    return o, p


def nsa_select(p_cmp, qpos, total_len):
    n_slc = -(-total_len // SLC_BLK)
    ratio = SLC_BLK // CMP_BLK
    imp = jnp.sum(p_cmp, axis=2)
    imp = jnp.pad(imp, ((0, 0), (0, 0), (0, 0), (0, n_slc * ratio - imp.shape[-1])))
    imp = imp.reshape(imp.shape[:-1] + (n_slc, ratio)).sum(-1)
    blk = jnp.arange(n_slc)[None, :]
    cur = (qpos // SLC_BLK)[:, None]
    valid = blk <= cur
    forced = (blk == 0) | (blk == cur) | (blk == cur - 1)
    score = jnp.where(valid & forced, SEL_FORCE, jnp.where(valid, imp, -SEL_FORCE))
    _, idx = lax.top_k(score, min(N_SEL, n_slc))
    return idx


def nsa_selected(q, qpos, idx, fetch, table):
    B, _, _, Tq, _ = q.shape
    n_sel = idx.shape[-1]
    qb = math.gcd(Tq, SLC_QBLK)
    nq = Tq // qb
    q_ch = jnp.moveaxis(q.reshape(B, KV_B, GRP_B, nq, qb, DH_B), 3, 0)
    pos_ch = qpos.reshape(nq, qb)
    idx_ch = jnp.moveaxis(idx.reshape(B, KV_B, nq, qb, n_sel), 2, 0)
    table_g = table.reshape(REL_BUCKETS, KV_B, GRP_B)
    gi = jnp.arange(KV_B)[None, :, None, None]
    offs = jnp.arange(SLC_BLK)

    def one_block(args):
        qc, pc, ic = args
        kpos = (ic[..., None] * SLC_BLK + offs).reshape(B, KV_B, qb, n_sel * SLC_BLK)
        kc, vc = fetch(kpos)
        dist = pc[None, None, :, None] - kpos
        bias = jnp.moveaxis(table_g[rel_bucket(dist), gi].astype(jnp.float32), -1, 2)
        s = jnp.einsum('bgrqd,bgqkd->bgrqk', qc, kc).astype(jnp.float32) * DH_B ** -0.5 + bias
        p = masked_softmax(s, (dist >= 0)[:, :, None])
        return jnp.einsum('bgrqk,bgqkd->bgrqd', p.astype(vc.dtype), vc)

    o = lax.map(one_block, (q_ch, pos_ch, idx_ch))
    return jnp.moveaxis(o, 0, 3).reshape(B, KV_B, GRP_B, Tq, DH_B)


NSA_TILE = 256
NSA_SUB_ROWS = 128
SLC_SHIFT = SLC_BLK.bit_length() - 1
assert 1 << SLC_SHIFT == SLC_BLK


def _nsa_flash_kernel(cfar_ref, q_ref, k_ref, v_ref, *rest, n_slc, window):
    if window:
        band_ref, o_ref, m_ref, l_ref, acc_ref = rest
    else:
        sel_ref, band_ref, o_ref, m_ref, l_ref, acc_ref = rest
        selt = sel_ref[0, 0].astype(jnp.bfloat16)
    g = pl.program_id(1)
    qi = pl.program_id(2)
    tq = tk = NSA_TILE
    bf16 = jnp.bfloat16
    f32 = jnp.float32
    qs = [q_ref[0, :, r * DH_B:(r + 1) * DH_B].astype(bf16) for r in range(GRP_B)]

    m_ref[...] = jnp.full(m_ref.shape, NEG, f32)
    l_ref[...] = jnp.zeros(l_ref.shape, f32)
    acc_ref[...] = jnp.zeros(acc_ref.shape, f32)

    def process(j, bias_of_head, triangle):
        k0 = pl.multiple_of(j * tk, tk)
        kt = k_ref[0, pl.ds(k0, tk), :].astype(bf16)
        vt = v_ref[0, pl.ds(k0, tk), :].astype(bf16)
        kt_t = k_ref[0, pl.ds(k0, tk), :].T.astype(bf16)
        mask = None
        if triangle is not None:
            row = lax.broadcasted_iota(jnp.int32, (tq, tk), 0)
            col = lax.broadcasted_iota(jnp.int32, (tq, tk), 1)
            mask = row >= col if triangle == "causal" else col > row
        if not window:
            key_blk = jnp.right_shift(k0 + lax.broadcasted_iota(jnp.int32, (n_slc, tk), 1), SLC_SHIFT)
            expand = (key_blk == lax.broadcasted_iota(jnp.int32, (n_slc, tk), 0)).astype(bf16)
            chosen = jnp.dot(selt, expand, preferred_element_type=f32) > 0.5
            mask = chosen if mask is None else jnp.logical_and(mask, chosen)
        mask_add = None if mask is None else jnp.where(mask, 0.0, NEG)
        for r in range(GRP_B):
            for r0 in range(0, tq, NSA_SUB_ROWS):
                rows = slice(r0, r0 + NSA_SUB_ROWS)
                s = jnp.dot(qs[r][rows], kt_t, preferred_element_type=f32)
                s = s * DH_B ** -0.5 + bias_of_head(r, rows)
                if mask is not None:
                    s = s + mask_add[rows]
                m_prev = m_ref[r, rows]
                m_next = jnp.maximum(m_prev, jnp.max(s, axis=-1, keepdims=True))
                p = jnp.exp(s - jnp.concatenate([m_next] * (tk // DH_B), axis=1))
                alpha = jnp.exp(m_prev - m_next)
                l_ref[r, rows] = alpha * l_ref[r, rows] + jnp.sum(p, axis=-1, keepdims=True)
                acc_ref[r, rows] = alpha * acc_ref[r, rows] + jnp.dot(p.astype(bf16), vt,
                                                                      preferred_element_type=f32)
                m_ref[r, rows] = m_next

    def far_bias(r, rows):
        return cfar_ref[g * GRP_B + r]

    if window:
        @pl.when(qi >= 2)
        def _():
            process(qi - 2, far_bias, "before")
    else:
        def far_tile(j, carry):
            process(j, far_bias, None)
            return carry

        lax.fori_loop(0, jnp.maximum(qi - 1, 0), far_tile, 0)

    @pl.when(qi >= 1)
    def _():
        process(qi - 1, lambda r, rows: band_ref[1, 0, r, rows], None)

    process(qi, lambda r, rows: band_ref[0, 0, r, rows], "causal")

    for r in range(GRP_B):
        o_ref[0, :, r * DH_B:(r + 1) * DH_B] = acc_ref[r] / l_ref[r]


def nsa_flash_contiguous(b_q, b_kv, table, branch, idx=None):
    B, T, _ = b_q.shape
    tq = tk = NSA_TILE
    window = idx is None
    assert T % tq == 0 and tk >= REL_MAX_DIST and tk % SLC_BLK == 0 and WINDOW == 2 * tk
    n_slc = -(-T // SLC_BLK)
    rel = jnp.arange(tq)[:, None] - jnp.arange(tk)[None, :]
    dist = jnp.stack([rel, rel + tk])
    band = table[rel_bucket(dist)].astype(jnp.float32).reshape(2, tq, tk, KV_B, GRP_B).transpose(0, 3, 4, 1, 2)
    cfar = table[rel_bucket(jnp.int32(2 * tk))].astype(jnp.float32)
    k_col = (branch * 2 + 0) * KV_B
    v_col = (branch * 2 + 1) * KV_B
    grp_w = GRP_B * DH_B
    sel_args, sel_specs = [], []
    if not window:
        sel_args = [jnp.any(idx[..., None] == jnp.arange(n_slc, dtype=idx.dtype), axis=-2).astype(jnp.float32)]
        sel_specs = [pl.BlockSpec((1, 1, tq, n_slc), lambda b, g, i: (b, g, i, 0))]
    return pl.pallas_call(
        functools.partial(_nsa_flash_kernel, n_slc=n_slc, window=window),
        out_shape=jax.ShapeDtypeStruct((B, T, W_B), jnp.float32),
        grid=(B, KV_B, T // tq),
        in_specs=[pl.BlockSpec(memory_space=pltpu.SMEM),
                  pl.BlockSpec((1, tq, grp_w), lambda b, g, i: (b, i, g)),
                  pl.BlockSpec((1, T, DH_B), lambda b, g, i: (b, 0, k_col + g)),
                  pl.BlockSpec((1, T, DH_B), lambda b, g, i: (b, 0, v_col + g))]
                 + sel_specs
                 + [pl.BlockSpec((2, 1, GRP_B, tq, tk), lambda b, g, i: (0, g, 0, 0, 0))],
        out_specs=pl.BlockSpec((1, tq, grp_w), lambda b, g, i: (b, i, g)),
        scratch_shapes=[pltpu.VMEM((GRP_B, tq, DH_B), jnp.float32),
                        pltpu.VMEM((GRP_B, tq, DH_B), jnp.float32),
                        pltpu.VMEM((GRP_B, tq, DH_B), jnp.float32)],
        compiler_params=pltpu.CompilerParams(
            dimension_semantics=("parallel", "parallel", "arbitrary"),
            vmem_limit_bytes=V7X_VMEM_LIMIT_BYTES),
        name="nsa_window" if window else "nsa_selected",
    )(cfar, b_q, b_kv, b_kv, *sel_args, band)


def nsa_window(q, qpos, kb, kpos, table):
    dist = qpos[:, :, None] - kpos[:, None, :]
    mask = (dist >= 0) & (dist < WINDOW) & (kpos[:, None, :] >= 0)
    s = (jnp.einsum('bgrnqd,bnkgd->bgrnqk', q, kb[:, :, :, 0]).astype(jnp.float32) * DH_B ** -0.5
         + rel_bias_heads(dist, table))
    p = masked_softmax(s, mask)
    return jnp.einsum('bgrnqk,bnkgd->bgrnqd', p.astype(kb.dtype), kb[:, :, :, 1])


def nsa_mixer(q_raw, qpos, gate_raw, pooled_cmp, total_len, fetch, win_blocks, win_kpos, cmp_w, table,
              contiguous_kv=None):
    B, Tq, _ = q_raw.shape
    q = q_raw.reshape(B, Tq, KV_B, GRP_B, DH_B).transpose(0, 2, 3, 1, 4)
    o_cmp, p_cmp = nsa_compressed(q, qpos, pooled_cmp, cmp_w, table, token_major=contiguous_kv is not None)
    idx = nsa_select(p_cmp, qpos, total_len)
    if contiguous_kv is not None:
        o_slc = nsa_flash_contiguous(q_raw, contiguous_kv, table, 1, idx).reshape(B, Tq, KV_B, GRP_B, DH_B)
        o_win = nsa_flash_contiguous(q_raw, contiguous_kv, table, 2).reshape(B, Tq, KV_B, GRP_B, DH_B)
        g = jax.nn.sigmoid(gate_raw.astype(jnp.float32).reshape(B, Tq, 3, KV_B, GRP_B, 1))
        o = g[:, :, 0] * o_cmp + g[:, :, 1] * o_slc + g[:, :, 2] * o_win
        return o.reshape(B, Tq, W_B).astype(q_raw.dtype)
    o_slc = fetch(q, qpos, idx)
    nq = win_blocks.shape[1]
    o_win = nsa_window(q.reshape(B, KV_B, GRP_B, nq, Tq // nq, DH_B), qpos.reshape(nq, Tq // nq),
                       win_blocks, win_kpos, table).reshape(B, KV_B, GRP_B, Tq, DH_B)
    g = jax.nn.sigmoid(gate_raw.astype(jnp.float32).reshape(B, Tq, 3, KV_B, GRP_B)).transpose(2, 0, 3, 4, 1)
    o = g[0][..., None] * o_cmp + g[1][..., None] * o_slc + g[2][..., None] * o_win
    return o.transpose(0, 3, 1, 2, 4).reshape(B, Tq, W_B).astype(q_raw.dtype)


def nsa_selected_paged(q, qpos, idx, pool_l, page_table, kv_new, table):
    B, _, _, Tq, _ = q.shape
    n_sel = idx.shape[-1]
    Tn = kv_new.shape[1]
    per_page = PAGE_SIZE // SLC_BLK
    n_past_blk = PAST_LEN // SLC_BLK
    assert PAGE_SIZE % SLC_BLK == 0 and PAST_LEN % SLC_BLK == 0
    kpos = idx[..., None] * SLC_BLK + jnp.arange(SLC_BLK)
    in_past = idx < n_past_blk
    blk = jnp.minimum(idx, n_past_blk - 1)
    bi = jnp.arange(B)[:, None, None, None]
    page = page_table[bi, blk // per_page]
    slices = pool_l.reshape(pool_l.shape[0] * per_page, SLC_BLK, 2, KV_B, DH_B)
    got = slices[page * per_page + blk % per_page]
    past_rows = jnp.einsum('bgtsrcgd->bgtsrcd', got).astype(kv_new.dtype)
    gi = jnp.arange(KV_B)[None, :, None, None, None]
    new_rows = kv_new[bi[..., None], jnp.clip(kpos - PAST_LEN, 0, Tn - 1), :, gi]
    rows = jnp.where(in_past[..., None, None, None], past_rows, new_rows)
    rows = rows.reshape(B, KV_B, Tq, n_sel * SLC_BLK, 2, DH_B)
    kc, vc = rows[..., 0, :], rows[..., 1, :]
    kpos = kpos.reshape(B, KV_B, Tq, n_sel * SLC_BLK)
    dist = qpos[None, None, :, None] - kpos
    table_g = table.reshape(REL_BUCKETS, KV_B, GRP_B)
    bias = jnp.moveaxis(table_g[rel_bucket(dist), jnp.arange(KV_B)[None, :, None, None]].astype(jnp.float32), -1, 2)
    s = jnp.einsum('bgrqd,bgqkd->bgrqk', q, kc).astype(jnp.float32) * DH_B ** -0.5 + bias
    p = masked_softmax(s, (dist >= 0)[:, :, None])
    return jnp.einsum('bgrqk,bgqkd->bgrqd', p.astype(vc.dtype), vc)


def contiguous_fetch(kv):
    bi = jnp.arange(kv.shape[0])[:, None, None, None]
    gi = jnp.arange(KV_B)[None, :, None, None]

    def fetch(kpos):
        rows = kv[bi, kpos, :, gi]
        return rows[..., 0, :], rows[..., 1, :]
    return fetch


def paged_fetch(pool, l, page_table, kv_new):
    B, Tn = kv_new.shape[:2]
    bi = jnp.arange(B)[:, None, None, None]
    gi = jnp.arange(KV_B)[None, :, None, None]
    li = jnp.full((1, 1, 1, 1), l, jnp.int32)

    def fetch(kpos):
        in_past = kpos < PAST_LEN
        pc = jnp.minimum(kpos, PAST_LEN - 1)
        page = page_table[bi, pc // PAGE_SIZE]
        past_rows = pool[li, page, pc % PAGE_SIZE, :, gi].astype(kv_new.dtype)
        new_rows = kv_new[bi, jnp.clip(kpos - PAST_LEN, 0, Tn - 1), :, gi]
        rows = jnp.where(in_past[..., None, None], past_rows, new_rows)
        return rows[..., 0, :], rows[..., 1, :]
    return fetch


MOE_ROWS = 512
MOE_UP_COLS = 512
MOE_DOWN_COLS = 1024


def _moe_up_kernel(blk_ref, ocol_ref, exp_ref, wcol_ref, flag_ref, x_ref, wg_ref, wu_ref, bg_ref, bu_ref,
                   h_ref, wg_bf, wu_bf):
    flag = flag_ref[pl.program_id(0)]

    @pl.when(flag >= 2)
    def _():
        @pl.when(flag == 3)
        def _():
            wg_bf[...] = wg_ref[...].astype(jnp.bfloat16)
            wu_bf[...] = wu_ref[...].astype(jnp.bfloat16)
        x = x_ref[...]
        gate = jnp.minimum(jnp.dot(x, wg_bf[...], preferred_element_type=jnp.float32) + bg_ref[...], SWIGLU_LIMIT)
        up = jnp.clip(jnp.dot(x, wu_bf[...], preferred_element_type=jnp.float32) + bu_ref[...],
                      -SWIGLU_LIMIT, SWIGLU_LIMIT)
        h_ref[...] = ((up + 1.0) * gate * jax.nn.sigmoid(SWIGLU_ALPHA * gate)).astype(h_ref.dtype)

    @pl.when(flag < 2)
    def _():
        h_ref[...] = jnp.zeros(h_ref.shape, h_ref.dtype)


def _moe_down_kernel(blk_ref, ocol_ref, exp_ref, wcol_ref, flag_ref, h_ref, wd_ref, bd_ref, y_ref, wd_bf):
    flag = flag_ref[pl.program_id(0)]

    @pl.when(flag >= 2)
    def _():
        @pl.when(flag == 3)
        def _():
            wd_bf[...] = wd_ref[...].astype(jnp.bfloat16)
        y_ref[...] = jnp.dot(h_ref[...], wd_bf[...], preferred_element_type=jnp.float32) + bd_ref[...]

    @pl.when(flag < 2)
    def _():
        y_ref[...] = jnp.zeros(y_ref.shape, y_ref.dtype)


def _moe_expert_matmul(body, nblk_e, first_blk, n_blocks, n_col, tn, rows, weights, biases, l, out_dtype, name):
    tm = MOE_ROWS
    K = rows[0].shape[1]
    N = n_col * tn
    n_items = n_blocks * n_col
    i = jnp.arange(n_items, dtype=jnp.int32)
    total_blocks = jnp.sum(nblk_e)
    total = total_blocks * n_col
    valid = i < total
    iv = jnp.minimum(i, total - 1)
    e = jnp.minimum(jnp.searchsorted(jnp.cumsum(nblk_e) * n_col, iv, side='right'), N_EXPERTS - 1).astype(jnp.int32)
    j = iv - first_blk[e] * n_col
    nb = jnp.maximum(nblk_e[e], 1)
    wcol = j // nb
    mb = j % nb
    spare = i - total
    blk = jnp.where(valid, first_blk[e] + mb, total_blocks + spare // n_col).astype(jnp.int32)
    ocol = jnp.where(valid, wcol, spare % n_col).astype(jnp.int32)
    flag = jnp.where(valid, 2 + (mb == 0), 0).astype(jnp.int32)
    wcol = wcol.astype(jnp.int32)

    w_spec = pl.BlockSpec((None, None, K, tn), lambda s, blk, ocol, e, wcol, flag: (l, e[s], 0, wcol[s]))
    b_spec = pl.BlockSpec((None, None, 1, tn), lambda s, blk, ocol, e, wcol, flag: (l, e[s], 0, wcol[s]))
    grid_spec = pltpu.PrefetchScalarGridSpec(
        num_scalar_prefetch=5,
        grid=(n_items,),
        in_specs=([pl.BlockSpec((tm, K), lambda s, blk, ocol, e, wcol, flag: (blk[s], 0))]
                  + [w_spec] * len(weights) + [b_spec] * len(biases)),
        out_specs=pl.BlockSpec((tm, tn), lambda s, blk, ocol, e, wcol, flag: (blk[s], ocol[s])),
        scratch_shapes=[pltpu.VMEM((K, tn), jnp.bfloat16)] * len(weights),
    )
    return pl.pallas_call(
        body,
        out_shape=jax.ShapeDtypeStruct((n_blocks * tm, N), out_dtype),
        grid_spec=grid_spec,
        compiler_params=pltpu.CompilerParams(dimension_semantics=("arbitrary",),
                                             vmem_limit_bytes=V7X_VMEM_LIMIT_BYTES),
        name=name,
    )(blk, ocol, e, wcol, flag, *rows, *weights,
      *[b.reshape(b.shape[0], b.shape[1], 1, b.shape[2]) for b in biases])


def moe_ffn(x, l, router_w, router_b, w_gate, b_gate, w_up, b_up, w_down, b_down):
    shape = x.shape
    xt = x.reshape(-1, D_MODEL)
    T = xt.shape[0]
    logits = jnp.dot(xt, router_w[l]).astype(jnp.float32) + router_b[l].astype(jnp.float32)
    top_val, top_exp = lax.top_k(logits, TOP_K)
    weights = jax.nn.softmax(top_val, axis=-1)
    n = T * TOP_K
    blk = MOE_ROWS
    flat_e = top_exp.reshape(-1)
    order = jnp.argsort(flat_e)
    e_sorted = flat_e[order]
    counts = jnp.zeros((N_EXPERTS,), jnp.int32).at[flat_e].add(1)
    nblk_e = (counts + blk - 1) // blk
    start = jnp.cumsum(counts) - counts
    first_blk = jnp.cumsum(nblk_e) - nblk_e
    dest = first_blk[e_sorted] * blk + jnp.arange(n, dtype=jnp.int32) - start[e_sorted]
    n_blocks = -(-(n + N_EXPERTS * (blk - 1)) // blk)
    slot_tok = jnp.full((n_blocks * blk,), T, jnp.int32).at[dest].set((order // TOP_K).astype(jnp.int32))
    slot_of = jnp.zeros((n,), jnp.int32).at[order].set(dest).reshape(T, TOP_K)
    x_ext = jnp.concatenate([xt.astype(jnp.bfloat16), jnp.zeros((1, D_MODEL), jnp.bfloat16)], axis=0)
    x_rows = x_ext[slot_tok]

    hb = _moe_expert_matmul(_moe_up_kernel, nblk_e, first_blk, n_blocks, D_FF // MOE_UP_COLS, MOE_UP_COLS,
                            [x_rows], [w_gate, w_up], [b_gate, b_up], l, jnp.bfloat16, "moe_gate_up")
    y_rows = _moe_expert_matmul(_moe_down_kernel, nblk_e, first_blk, n_blocks, D_MODEL // MOE_DOWN_COLS,
                                MOE_DOWN_COLS, [hb], [w_down], [b_down], l, jnp.float32, "moe_down")
    y = jnp.sum(y_rows[slot_of] * weights[..., None], axis=1)
    return y.astype(x.dtype).reshape(shape)


def token_mixers(h, l, qpos, prm, past):
    B, T, _ = h.shape
    f32 = jnp.float32
    proj = _dot2(h, prm['w_in'][l])
    (a_qkv, a_z, a_a, a_b, b_q, b_kv, b_g, c_q, c_k, c_v, c_if, c_o, merge_g) = jnp.split(proj, IN_OFFSETS, axis=-1)
    if past is None:
        conv_prev = jnp.zeros((B, CONV_W - 1, 3 * W_A), h.dtype)
        S0 = jnp.zeros((B, H_A, DK_A, DV_A), f32)
        C0 = jnp.zeros((B, H_C, DK_C, DV_C), f32)
        n0 = jnp.zeros((B, H_C, DK_C), f32)
        m0 = jnp.full((B, H_C), NEG, f32)
    else:
        conv_prev = past['gdn_conv'][l]
        S0 = past['gdn_S'][l]
        C0 = past['mlstm_C'][l]
        n0 = past['mlstm_n'][l]
        m0 = past['mlstm_m'][l]

    o_a, S_a, conv_a = gdn_mixer(a_qkv, a_z, a_a, a_b, conv_prev, S0, prm['gdn_conv_w'][l],
                                 prm['gdn_a_log'][l], prm['gdn_dt_bias'][l], prm['gdn_norm_w'][l])

    kv_new = b_kv.reshape(B, T, 3, 2, KV_B, DH_B)
    kv_cmp, kv_slc, kv_win = kv_new[:, :, 0], kv_new[:, :, 1], kv_new[:, :, 2]
    cmp_pe = prm['nsa_cmp_pe'][l]
    if past is None:
        total_len = T
        pooled_cmp = cmp_pool(kv_cmp[:, :T // CMP_BLK * CMP_BLK], cmp_pe)
        fetch = win_blocks = win_kpos = None
        win_all = kv_win
    else:
        assert PAGE_SIZE % CMP_BLK == 0 and PAST_LEN % PAGE_SIZE == 0
        total_len = PAST_LEN + T
        page_pooled = cmp_pool(past['kv_cmp'][l].astype(kv_cmp.dtype), cmp_pe)
        pooled_cmp = page_pooled[past['page_table']].reshape(B, PAST_LEN // CMP_BLK, 2, KV_B, DH_B)
        n_new_blocks = total_len // CMP_BLK - PAST_LEN // CMP_BLK
        if n_new_blocks:
            pooled_cmp = jnp.concatenate([pooled_cmp, cmp_pool(kv_cmp[:, :n_new_blocks * CMP_BLK], cmp_pe)], axis=1)
        fetch = functools.partial(nsa_selected_paged, pool_l=past['kv_slc'][l], page_table=past['page_table'],
                                  kv_new=kv_slc, table=prm['rel_bias_table'])
        buf = past['kv_win'][l].astype(kv_win.dtype)
        wb = buf.shape[1]
        win_all = jnp.concatenate([buf, kv_win], axis=1)
        win_blocks = win_all[:, None]
        win_kpos = (PAST_LEN - wb + jnp.arange(wb + T))[None, :]
    o_b = nsa_mixer(b_q, qpos, b_g, pooled_cmp, total_len, fetch, win_blocks, win_kpos,
                    prm['nsa_cmp_w'][l], prm['rel_bias_table'],
                    contiguous_kv=b_kv if past is None else None)
    new_win = win_all[:, -min(WINDOW, win_all.shape[1]):]

    o_c, C, n, m = mlstm_mixer(c_q, c_k, c_v, c_if, c_o, C0, n0, m0, prm['mlstm_b_if'][l], prm['mlstm_norm_w'][l])

    merged = merge_up(o_a.reshape(B * T, W_A), o_b.reshape(B * T, W_B), o_c.reshape(B * T, W_C),
                      merge_g.astype(f32).reshape(B * T, 3 * D_MODEL),
                      prm['w_up_a'][l], prm['w_up_b'][l], prm['w_up_c'][l])
    out = _dot2(merged.reshape(B, T, D_MODEL), prm['w_out'][l])
    return out, (kv_cmp, kv_slc, new_win, S_a, conv_a, C, n, m)


def kernel(x_prompt, x_sample, cache_kv_cmp, cache_kv_slc, cache_kv_win, state_gdn_S, state_gdn_conv,
           state_mlstm_C, state_mlstm_n, state_mlstm_m, page_table,
           norm_mix, norm_ffn, norm_final, w_in, gdn_conv_w, gdn_a_log, gdn_dt_bias, gdn_norm_w,
           nsa_cmp_pe, nsa_cmp_w, rel_bias_table, mlstm_b_if, mlstm_norm_w,
           w_up_a, w_up_b, w_up_c, w_out, router_w, router_b,
           exp_w_gate, exp_b_gate, exp_w_up, exp_b_up, exp_w_down, exp_b_down):
    prm = {'w_in': w_in, 'gdn_conv_w': gdn_conv_w, 'gdn_a_log': gdn_a_log, 'gdn_dt_bias': gdn_dt_bias,
           'gdn_norm_w': gdn_norm_w, 'nsa_cmp_pe': nsa_cmp_pe, 'nsa_cmp_w': nsa_cmp_w,
           'rel_bias_table': rel_bias_table, 'mlstm_b_if': mlstm_b_if, 'mlstm_norm_w': mlstm_norm_w,
           'w_up_a': w_up_a, 'w_up_b': w_up_b, 'w_up_c': w_up_c, 'w_out': w_out}
    past = {'kv_cmp': cache_kv_cmp, 'kv_slc': cache_kv_slc, 'kv_win': cache_kv_win, 'gdn_S': state_gdn_S,
            'gdn_conv': state_gdn_conv, 'mlstm_C': state_mlstm_C, 'mlstm_n': state_mlstm_n,
            'mlstm_m': state_mlstm_m, 'page_table': page_table}

    qpos_p = jnp.arange(SEQ, dtype=jnp.int32)
    qpos_s = PAST_LEN + jnp.arange(DEC_SEQ, dtype=jnp.int32)
    n_p = BATCH * SEQ
    xp, xs = x_prompt, x_sample
    states_p, states_s = [], []
    for l in range(DEPTH):
        mix_p, st_p = token_mixers(rms_norm(xp, norm_mix[l]), l, qpos_p, prm, None)
        mix_s, st_s = token_mixers(rms_norm(xs, norm_mix[l]), l, qpos_s, prm, past)
        xp = xp + mix_p
        xs = xs + mix_s
        h_all = jnp.concatenate([rms_norm(xp, norm_ffn[l]).reshape(n_p, D_MODEL),
                                 rms_norm(xs, norm_ffn[l]).reshape(-1, D_MODEL)], axis=0)
        y_all = moe_ffn(h_all, l, router_w, router_b, exp_w_gate, exp_b_gate,
                        exp_w_up, exp_b_up, exp_w_down, exp_b_down)
        xp = xp + y_all[:n_p].reshape(xp.shape)
        xs = xs + y_all[n_p:].reshape(xs.shape)
        states_p.append(st_p)
        states_s.append(st_s)
    y_prompt = rms_norm(xp, norm_final)
    y_sample = rms_norm(xs, norm_final)
    kvc_p, kvs_p, win_p, gS_p, gconv_p, mC_p, mn_p, mm_p = [jnp.stack(s, axis=0) for s in zip(*states_p)]
    kvc_s, kvs_s, win_s, gS_s, gconv_s, mC_s, mn_s, mm_s = [jnp.stack(s, axis=0) for s in zip(*states_s)]
    return (y_prompt, y_sample, kvc_p, kvc_s, kvs_p, kvs_s, win_p, win_s, gS_p, gS_s,
            gconv_p, gconv_s, mC_p, mC_s, mn_p, mn_s, mm_p, mm_s)
```
